```python
import jax
import jax.numpy as jnp
from jax import lax
import numpy as np


D_MODEL = 1024
BATCH = 4
SEQ = 8192
DEPTH = 2

CHUNK = 64
N_BRANCH = 4
BRANCH_W = 512
CONV_CH = BRANCH_W
CONV_WIDTH = 31
SGU_CH = BRANCH_W
SGU_GROUPS = 4
SGU_WIN = 128
HG_HEADS = 4
HG_DK = 128
HG_DV = BRANCH_W // HG_HEADS
HG_W = HG_HEADS * HG_DK
RW_HEADS = 8
RW_N = 64
RW_W = RW_HEADS * RW_N
RW_DECAY_LORA = 64
RW_AAA_LORA = 64
RW_MV_LORA = 32
RW_GATE_LORA = 160
FFN_HIDDEN = ((8 * D_MODEL + 3 * 256 - 1) // (3 * 256)) * 256
IN_SIZES = (CONV_CH, CONV_CH, SGU_CH, SGU_CH, HG_W, HG_W, HG_HEADS * HG_DV, HG_HEADS * HG_DV, RW_W, RW_W, RW_W)
IN_COLS = sum(IN_SIZES)
RMS_EPS = 1e-6
LN_EPS = 1e-5
RW_GN_EPS = 64e-5

kernel_name = 'gated_hybrid_streaming_encoder'


def _rms_norm(x, g, eps=RMS_EPS):
    xf = x.astype(jnp.float32)
    y = xf * lax.rsqrt(jnp.mean(xf * xf, axis=-1, keepdims=True) + eps)
    return (y * g.astype(jnp.float32)).astype(x.dtype)


def _layer_norm(x, g, b, eps=LN_EPS):
    xf = x.astype(jnp.float32)
    xc = xf - jnp.mean(xf, axis=-1, keepdims=True)
    y = xc * lax.rsqrt(jnp.mean(xc * xc, axis=-1, keepdims=True) + eps)
    return (y * g.astype(jnp.float32) + b.astype(jnp.float32)).astype(x.dtype)


def _token_shift(x):
    return jnp.pad(x, ((0, 0), (1, 0), (0, 0)))[:, :-1]


def _conformer_conv(p_val, p_gate, conv_w, conv_b, ln_g, ln_b):
    a = p_val * jax.nn.sigmoid(p_gate)
    a = lax.conv_general_dilated(
        a, conv_w[:, None, :], window_strides=(1,),
        padding=((CONV_WIDTH - 1, 0),),
        dimension_numbers=('NWC', 'WIO', 'NWC'),
        feature_group_count=CONV_CH) + conv_b
    return jax.nn.silu(_layer_norm(a, ln_g, ln_b))


def _spatial_gating(p_u, p_v, ln_g, ln_b, w_sp, b_sp):
    bsz, seq, _ = p_u.shape
    u = jax.nn.gelu(p_u, approximate=False)
    v = _layer_norm(jax.nn.gelu(p_v, approximate=False), ln_g, ln_b)
    v = v.reshape(bsz, seq // SGU_WIN, SGU_WIN, SGU_GROUPS, SGU_CH // SGU_GROUPS)
    blk = jnp.arange(SGU_WIN) // CHUNK
    allowed = blk[None, :] <= blk[:, None]
    w = jnp.where(allowed[None], w_sp, jnp.zeros_like(w_sp))
    mixed = jnp.einsum('gij,bwjgc->bwigc', w, v) + b_sp.T[:, :, None]
    return u * mixed.reshape(bsz, seq, SGU_CH)


def _hgrn2(p_q, p_f, p_i, p_g, lower_bound, norm_g):
    f32 = jnp.float32
    bsz, seq, _ = p_q.shape
    n_chunks = seq // CHUNK
    q = jax.nn.silu(p_q.astype(f32))
    log_f = jnp.logaddexp(jnp.log(lower_bound), jnp.log1p(-lower_bound) + jax.nn.log_sigmoid(p_f.astype(f32)))
    k = -jnp.expm1(log_f)
    v = p_i.astype(f32)

    def to_chunks(t, d):
        return t.reshape(bsz, n_chunks, CHUNK, HG_HEADS, d).transpose(1, 0, 3, 2, 4)

    causal = jnp.tril(jnp.ones((CHUNK, CHUNK), dtype=bool))[:, :, None]

    def chunk_step(state, inp):
        qc, kc, vc, lc = inp
        b = jnp.cumsum(lc, axis=2)
        diff = b[:, :, :, None, :] - b[:, :, None, :, :]
        decay = jnp.exp(jnp.where(causal, diff, -jnp.inf))
        scores = jnp.einsum('bhtk,bhsk,bhtsk->bhts', qc, kc, decay)
        out = (jnp.einsum('bhts,bhsv->bhtv', scores, vc)
               + jnp.einsum('bhtk,bhkv->bhtv', qc * jnp.exp(b), state))
        b_end = b[:, :, -1:, :]
        state = (jnp.exp(b_end)[:, :, 0, :, None] * state
                 + jnp.einsum('bhsk,bhsv->bhkv', kc * jnp.exp(b_end - b), vc))
        return state, out

    state0 = jnp.zeros((bsz, HG_HEADS, HG_DK, HG_DV), f32)
    _, o = lax.scan(chunk_step, state0,
                    (to_chunks(q, HG_DK), to_chunks(k, HG_DK), to_chunks(v, HG_DV), to_chunks(log_f, HG_DK)))
    o = o.transpose(1, 0, 3, 2, 4).reshape(bsz, seq, HG_HEADS, HG_DV)
    o = o * lax.rsqrt(jnp.mean(o * o, axis=-1, keepdims=True) + RMS_EPS)
    o = o.reshape(bsz, seq, HG_HEADS * HG_DV) * norm_g.astype(f32) * jax.nn.silu(p_g.astype(f32))
    return o.astype(p_q.dtype)


def _rwkv7(h, p_r, p_k, p_v, v_first, mu_rkv, mu_lora, w0, w1, w2, a0, a1, a2,
           g1, g2, k_k, k_a, r_k, ln_g, ln_b, vres):
    f32 = jnp.float32
    bsz, seq, _ = h.shape
    dh = _token_shift(h) - h
    xw = h + dh * mu_lora[0]
    xa = h + dh * mu_lora[1]
    xg = h + dh * mu_lora[2]
    r = p_r + (_token_shift(p_r) - p_r) * mu_rkv[0]
    k = p_k + (_token_shift(p_k) - p_k) * mu_rkv[1]
    v = p_v + (_token_shift(p_v) - p_v) * mu_rkv[2]
    w_log = -jax.nn.softplus(-(w0 + jnp.tanh(xw @ w1) @ w2).astype(f32)) - 0.5
    decay = jnp.exp(-jnp.exp(w_log))
    if vres is None:
        v_first = v
    else:
        mu_v, v0, v1, v2 = vres
        xv = h + dh * mu_v
        v = v + (v_first - v) * jax.nn.sigmoid(v0 + (xv @ v1) @ v2)
    a = jax.nn.sigmoid((a0 + (xa @ a1) @ a2).astype(f32))
    gate = jax.nn.sigmoid(xg @ g1) @ g2
    kf = k.astype(f32)
    kk = (kf * k_k).reshape(bsz, seq, RW_HEADS, RW_N)
    kk = kk * lax.rsqrt(jnp.maximum(jnp.sum(kk * kk, axis=-1, keepdims=True), 1e-24))
    kf = kf * (1.0 + (a - 1.0) * k_a)
    rf = r.astype(f32)
    vf = v.astype(f32)

    def heads(t):
        return t.reshape(bsz, seq, RW_HEADS, RW_N)

    def seq_major(t):
        return heads(t).transpose(1, 0, 2, 3)

    def time_step(state, inp):
        r_t, w_t, k_t, v_t, kk_t, a_t = inp
        s_kk = jnp.einsum('bhvk,bhk->bhv', state, kk_t)
        state = (state * w_t[:, :, None, :]
                 - s_kk[..., None] * (kk_t * a_t)[:, :, None, :]
                 + v_t[..., None] * k_t[:, :, None, :])
        return state, jnp.einsum('bhvk,bhk->bhv', state, r_t)

    state0 = jnp.zeros((bsz, RW_HEADS, RW_N, RW_N), f32)
    _, y = lax.scan(time_step, state0,
                    (seq_major(rf), seq_major(decay), seq_major(kf), seq_major(vf),
                     kk.transpose(1, 0, 2, 3), seq_major(a)))
    y = y.transpose(1, 0, 2, 3)
    yc = y - jnp.mean(y, axis=-1, keepdims=True)
    y = yc * lax.rsqrt(jnp.mean(yc * yc, axis=-1, keepdims=True) + RW_GN_EPS)
    y = y * ln_g.astype(f32).reshape(RW_HEADS, RW_N) + ln_b.astype(f32).reshape(RW_HEADS, RW_N)
    y = y + jnp.sum(heads(rf) * heads(kf) * r_k.astype(f32), axis=-1, keepdims=True) * heads(vf)
    y = y.reshape(bsz, seq, RW_W) * gate.astype(f32)
    return y.astype(h.dtype), v_first


def setup_inputs(seed: int = 0) -> dict:
    key = jax.random.key(seed)
    ks = iter(jax.random.split(key, 48))
    D = D_MODEL

    def nrm(shape, scale):
        return scale * jax.random.normal(next(ks), shape, jnp.float32)

    def uni(shape, lo, hi):
        return jax.random.uniform(next(ks), shape, jnp.float32, lo, hi)

    def gain(shape):
        return 1.0 + nrm(shape, 0.05)

    return {
        'x': nrm((BATCH, SEQ, D), 1.0),
        'norm_mix_g': gain((DEPTH, D)),
        'w_in': nrm((DEPTH, D, IN_COLS), D ** -0.5),
        'w_gate': nrm((DEPTH, N_BRANCH, D, D), D ** -0.5),
        'b_gate': nrm((DEPTH, N_BRANCH, D), 0.1),
        'conv_w': nrm((DEPTH, CONV_WIDTH, CONV_CH), CONV_WIDTH ** -0.5),
        'conv_b': nrm((DEPTH, CONV_CH), 0.02),
        'conv_ln_g': gain((DEPTH, CONV_CH)),
        'conv_ln_b': nrm((DEPTH, CONV_CH), 0.02),
        'sgu_ln_g': gain((DEPTH, SGU_CH)),
        'sgu_ln_b': nrm((DEPTH, SGU_CH), 0.02),
        'sgu_w': nrm((DEPTH, SGU_GROUPS, SGU_WIN, SGU_WIN), SGU_WIN ** -0.5),
        'sgu_b': 1.0 + nrm((DEPTH, SGU_GROUPS, SGU_WIN), 0.05),
        'hg_lb': nrm((DEPTH, HG_W), 1.0),
        'hg_norm_g': gain((DEPTH, HG_HEADS * HG_DV)),
        'rw_mu_rkv': uni((DEPTH, 3, RW_W), 0.0, 1.0),
        'rw_mu_lora': uni((DEPTH, 3, D), 0.0, 1.0),
        'rw_w0': uni((DEPTH, RW_W), -6.0, 0.0),
        'rw_w1': nrm((DEPTH, D, RW_DECAY_LORA), D ** -0.5),
        'rw_w2': nrm((DEPTH, RW_DECAY_LORA, RW_W), RW_DECAY_LORA ** -0.5),
        'rw_a0': nrm((DEPTH, RW_W), 0.1),
        'rw_a1': nrm((DEPTH, D, RW_AAA_LORA), D ** -0.5),
        'rw_a2': nrm((DEPTH, RW_AAA_LORA, RW_W), RW_AAA_LORA ** -0.5),
        'rw_g1': nrm((DEPTH, D, RW_GATE_LORA), D ** -0.5),
        'rw_g2': nrm((DEPTH, RW_GATE_LORA, RW_W), RW_GATE_LORA ** -0.5),
        'rw_k_k': 0.85 + nrm((DEPTH, RW_W), 0.05),
        'rw_k_a': 1.0 + nrm((DEPTH, RW_W), 0.05),
        'rw_r_k': nrm((DEPTH, RW_HEADS, RW_N), 0.1),
        'rw_ln_g': gain((DEPTH, RW_W)),
        'rw_ln_b': nrm((DEPTH, RW_W), 0.02),
        'rw_mu_vres': uni((DEPTH - 1, D), 0.0, 1.0),
        'rw_v0': 1.0 + nrm((DEPTH - 1, RW_W), 0.1),
        'rw_v1': nrm((DEPTH - 1, D, RW_MV_LORA), D ** -0.5),
        'rw_v2': nrm((DEPTH - 1, RW_MV_LORA, RW_W), RW_MV_LORA ** -0.5),
        'w_branch': nrm((DEPTH, N_BRANCH, BRANCH_W, D), BRANCH_W ** -0.5),
        'w_out': nrm((DEPTH, D, D), D ** -0.5),
        'norm_ffn_g': gain((DEPTH, D)),
        'w_ffn_gate': nrm((DEPTH, D, FFN_HIDDEN), D ** -0.5),
        'w_ffn_up': nrm((DEPTH, D, FFN_HIDDEN), D ** -0.5),
        'w_ffn_down': nrm((DEPTH, FFN_HIDDEN, D), FFN_HIDDEN ** -0.5),
        'final_norm_g': gain((D,)),
    }


def reference(x, norm_mix_g, w_in, w_gate, b_gate, conv_w, conv_b, conv_ln_g, conv_ln_b,
              sgu_ln_g, sgu_ln_b, sgu_w, sgu_b, hg_lb, hg_norm_g,
              rw_mu_rkv, rw_mu_lora, rw_w0, rw_w1, rw_w2, rw_a0, rw_a1, rw_a2,
              rw_g1, rw_g2, rw_k_k, rw_k_a, rw_r_k, rw_ln_g, rw_ln_b,
              rw_mu_vres, rw_v0, rw_v1, rw_v2,
              w_branch, w_out, norm_ffn_g, w_ffn_gate, w_ffn_up, w_ffn_down, final_norm_g):
    offsets = [int(o) for o in np.cumsum(IN_SIZES)[:-1]]
    lbs = jnp.cumsum(jax.nn.softmax(hg_lb.astype(jnp.float32), axis=0), axis=0)
    lbs = lbs - lbs[0]
    v_first = None
    for l in range(DEPTH):
        h = _rms_norm(x, norm_mix_g[l])
        p = h @ w_in[l]
        (pa_val, pa_gate, pb_u, pb_v, pc_q, pc_f, pc_i, pc_g,
         pd_r, pd_k, pd_v) = jnp.split(p, offsets, axis=-1)
        y_a = _conformer_conv(pa_val, pa_gate, conv_w[l], conv_b[l], conv_ln_g[l], conv_ln_b[l])
        y_b = _spatial_gating(pb_u, pb_v, sgu_ln_g[l], sgu_ln_b[l], sgu_w[l], sgu_b[l])
        y_c = _hgrn2(pc_q, pc_f, pc_i, pc_g, lbs[l], hg_norm_g[l])
        vres = None if l == 0 else (rw_mu_vres[l - 1], rw_v0[l - 1], rw_v1[l - 1], rw_v2[l - 1])
        y_d, v_first = _rwkv7(h, pd_r, pd_k, pd_v, v_first, rw_mu_rkv[l], rw_mu_lora[l],
                              rw_w0[l], rw_w1[l], rw_w2[l], rw_a0[l], rw_a1[l], rw_a2[l],
                              rw_g1[l], rw_g2[l], rw_k_k[l], rw_k_a[l], rw_r_k[l],
                              rw_ln_g[l], rw_ln_b[l], vres)
        mixed = jnp.zeros_like(x)
        for j, y_j in enumerate((y_a, y_b, y_c, y_d)):
            g_j = jax.nn.sigmoid(h @ w_gate[l, j] + b_gate[l, j])
            mixed = mixed + g_j * (y_j @ w_branch[l, j])
        x = x + mixed @ w_out[l]
        h = _rms_norm(x, norm_ffn_g[l])
        x = x + (jax.nn.silu(h @ w_ffn_gate[l]) * (h @ w_ffn_up[l])) @ w_ffn_down[l]
    return _rms_norm(x, final_norm_g)
```

```python
import functools

import jax
import jax.numpy as jnp
from jax import lax
from jax.experimental import pallas as pl
from jax.experimental.pallas import tpu as pltpu

F32 = jnp.float32
BF16 = jnp.bfloat16

CHUNK = 64
SUB = 16
BRANCH_W = 512
CONV_WIDTH = 31
CONV_HALO = 32
SGU_GROUPS = 4
SGU_WIN = 128
HG_HEADS = 4
HG_DK = 128
RW_HEADS = 8
RW_N = 64
RMS_EPS = 1e-6
LN_EPS = 1e-5
RW_GN_EPS = 64e-5
VMEM_LIMIT = 56 * 1024 * 1024


def _bf(x):
    return x.astype(BF16)


def _dot(a, b):
    return jnp.dot(a, b, preferred_element_type=F32)


def _dot_nt(a, b):
    return lax.dot_general(a, b, (((1,), (1,)), ((), ())), preferred_element_type=F32)


def _dot_tn(a, b):
    return lax.dot_general(a, b, (((0,), (0,)), ((), ())), preferred_element_type=F32)


def _split(x, n):
    parts = []
    r = x
    for _ in range(n - 1):
        p = _bf(r)
        parts.append(p)
        r = r - p.astype(F32)
    parts.append(_bf(r))
    return parts


def _mm(a, b, passes, dot=_dot):
    if passes == 1:
        return dot(_bf(a), _bf(b))
    n = 2 if passes == 3 else 3
    ap, bp = _split(a, n), _split(b, n)
    acc = None
    for i in range(n):
        for j in range(n):
            if i + j < n:
                t = dot(ap[i], bp[j])
                acc = t if acc is None else acc + t
    return acc


def _mm_exact_lhs(a_bf, b, n, dot=_dot):
    acc = None
    for p in _split(b, n):
        t = dot(a_bf, p)
        acc = t if acc is None else acc + t
    return acc


def _mm_exact_rhs(a, b_bf, n):
    acc = None
    for p in _split(a, n):
        t = _dot(p, b_bf)
        acc = t if acc is None else acc + t
    return acc


def _rms(x, g, eps=RMS_EPS):
    return x * lax.rsqrt(jnp.mean(x * x, axis=-1, keepdims=True) + eps) * g


def _layer_norm(x, g, b, eps=LN_EPS):
    xc = x - jnp.mean(x, axis=-1, keepdims=True)
    return xc * lax.rsqrt(jnp.mean(xc * xc, axis=-1, keepdims=True) + eps) * g + b


def _sigmoid(x):
    return 1.0 / (1.0 + jnp.exp(-x))


def _silu(x):
    return x * _sigmoid(x)


def _gelu(x):
    return 0.5 * x * (1.0 + lax.erf(x * 0.7071067811865476))


def _softplus(x):
    return jnp.maximum(x, 0.0) + jnp.log1p(jnp.exp(-jnp.abs(x)))


def _iota2(shape, dim):
    return lax.broadcasted_iota(jnp.int32, shape, dim)


def _resident(shape):
    nd = len(shape)
    return pl.BlockSpec(shape, lambda *_: (0,) * nd)


def _params(*sem):
    return pltpu.CompilerParams(dimension_semantics=sem, vmem_limit_bytes=VMEM_LIMIT)


def _ab_kernel(x_ref, g_ref, w_ref, lng_ref, lnb_ref, wsp_ref, bsp_ref, a_ref, yb_ref, *, tm):
    h = _bf(_rms(x_ref[...], g_ref[...]))
    pa = _dot(h, w_ref[:, 0:2 * BRANCH_W])
    a_ref[...] = pa[:, :BRANCH_W] * _sigmoid(pa[:, BRANCH_W:])
    pb = _dot(h, w_ref[:, 2 * BRANCH_W:4 * BRANCH_W])
    u = _gelu(pb[:, :BRANCH_W])
    v = _bf(_layer_norm(_gelu(pb[:, BRANCH_W:]), lng_ref[...], lnb_ref[...]))
    row = _iota2((SGU_WIN, SGU_WIN), 0)
    col = _iota2((SGU_WIN, SGU_WIN), 1)
    allowed = (col // CHUNK) <= (row // CHUNK)
    gw = BRANCH_W // SGU_GROUPS
    for g in range(SGU_GROUPS):
        wg = _bf(jnp.where(allowed, wsp_ref[g], 0.0))
        bg = bsp_ref[g]
        for w in range(tm // SGU_WIN):
            rs = slice(w * SGU_WIN, (w + 1) * SGU_WIN)
            cs = slice(g * gw, (g + 1) * gw)
            mixed = _dot(wg, v[rs, cs]) + bg
            yb_ref[rs, cs] = (u[rs, cs] * mixed).astype(yb_ref.dtype)


def _proj_ab(x2, g, w_ab, sgu_ln_g, sgu_ln_b, sgu_w, sgu_b, tm):
    T, D = x2.shape
    kern = functools.partial(_ab_kernel, tm=tm)
    return pl.pallas_call(
        kern,
        grid=(T // tm,),
        in_specs=[
            pl.BlockSpec((tm, D), lambda i: (i, 0)),
            _resident((1, D)),
            _resident(w_ab.shape),
            _resident((1, BRANCH_W)),
            _resident((1, BRANCH_W)),
            _resident(sgu_w.shape),
            _resident((SGU_GROUPS, SGU_WIN, 1)),
        ],
        out_specs=[
            pl.BlockSpec((tm, BRANCH_W), lambda i: (i, 0)),
            pl.BlockSpec((tm, BRANCH_W), lambda i: (i, 0)),
        ],
        out_shape=[
            jax.ShapeDtypeStruct((T, BRANCH_W), F32),
            jax.ShapeDtypeStruct((T, BRANCH_W), BF16),
        ],
        compiler_params=_params("parallel"),
        name="proj_ab",
    )(x2, g.reshape(1, D), w_ab, sgu_ln_g.reshape(1, -1), sgu_ln_b.reshape(1, -1), sgu_w,
      sgu_b.reshape(SGU_GROUPS, SGU_WIN, 1))


def _conv_kernel(a_ref, halo_ref, w_ref, b_ref, lng_ref, lnb_ref, o_ref, buf_ref, *, tm, seq, rb):
    first = (pl.program_id(0) * tm) % seq == 0
    buf_ref[0:CONV_HALO, :] = jnp.where(first, 0.0, halo_ref[...])
    buf_ref[CONV_HALO:, :] = a_ref[...]
    off = CONV_HALO - (CONV_WIDTH - 1)
    for r in range(tm // rb):
        acc = jnp.zeros((rb, BRANCH_W), F32) + b_ref[...]
        for j in range(CONV_WIDTH):
            acc = acc + w_ref[j:j + 1, :] * buf_ref[pl.ds(r * rb + off + j, rb), :]
        y = _layer_norm(acc, lng_ref[...], lnb_ref[...])
        o_ref[r * rb:(r + 1) * rb, :] = _silu(y).astype(o_ref.dtype)


def _conv(a, conv_w, conv_b, ln_g, ln_b, seq, tm):
    T, C = a.shape
    kern = functools.partial(_conv_kernel, tm=tm, seq=seq, rb=64)
    hb = tm // CONV_HALO
    return pl.pallas_call(
        kern,
        grid=(T // tm,),
        in_specs=[
            pl.BlockSpec((tm, C), lambda i: (i, 0)),
            pl.BlockSpec((CONV_HALO, C), lambda i: (jnp.maximum(i * hb - 1, 0), 0)),
            _resident(conv_w.shape),
            _resident((1, C)),
            _resident((1, C)),
            _resident((1, C)),
        ],
        out_specs=pl.BlockSpec((tm, C), lambda i: (i, 0)),
        out_shape=jax.ShapeDtypeStruct((T, C), BF16),
        scratch_shapes=[pltpu.VMEM((tm + CONV_HALO, C), F32)],
        compiler_params=_params("parallel"),
        name="conv",
    )(a, a, conv_w, conv_b.reshape(1, C), ln_g.reshape(1, C), ln_b.reshape(1, C))


def _hg_kernel(x_ref, g_ref, w_ref, lb_ref, ng_ref, o_ref,
               q_s, lf_s, k_s, v_s, og_s, st_ref, *, tm, layer, depth):
    @pl.when(pl.program_id(1) == 0)
    def _():
        st_ref[...] = jnp.zeros_like(st_ref)

    W = HG_HEADS * HG_DK
    h = _bf(_rms(x_ref[0], g_ref[...]))
    p = _dot(h, w_ref[...])
    z = p[:, W:2 * W]
    lbp = lb_ref[...]
    e = jnp.exp(lbp - jnp.max(lbp, axis=0, keepdims=True))
    sm = e / jnp.sum(e, axis=0, keepdims=True)
    cs0 = sm[0:1]
    cs = cs0
    for l in range(1, layer + 1):
        cs = cs + sm[l:l + 1]
    lb = cs - cs0
    log_sig = -_softplus(-z)
    t1 = jnp.log(lb)
    t2 = jnp.log1p(-lb) + log_sig
    log_f = jnp.maximum(t1, t2) + jnp.log1p(jnp.exp(-jnp.abs(t1 - t2)))
    q_s[...] = _silu(p[:, 0:W])
    lf_s[...] = log_f
    k_s[...] = (1.0 - lb) * _sigmoid(-z)
    v_s[...] = p[:, 2 * W:3 * W]
    og_s[...] = _silu(p[:, 3 * W:4 * W]) * ng_ref[...]

    tri = _bf((_iota2((CHUNK, CHUNK), 1) <= _iota2((CHUNK, CHUNK), 0)).astype(F32))
    nsub = CHUNK // SUB
    rowc = _iota2((CHUNK, HG_DK), 0)
    row_s = _iota2((SUB, 1), 0)
    lane_c = _iota2((SUB, CHUNK), 1)

    def chunk(c, carry):
        r0 = pl.multiple_of(c * CHUNK, CHUNK)
        rows = pl.ds(r0, CHUNK)
        for hd in range(HG_HEADS):
            cs_ = slice(hd * HG_DK, (hd + 1) * HG_DK)
            q = q_s[rows, cs_]
            lf = lf_s[rows, cs_]
            k = k_s[rows, cs_]
            v = _bf(v_s[rows, cs_])
            b = _mm_exact_lhs(tri, lf, 3)
            bend = jnp.concatenate(
                [jnp.broadcast_to(b[(j + 1) * SUB - 1:(j + 1) * SUB, :], (SUB, HG_DK)) for j in range(nsub)], axis=0)
            khat = k * jnp.exp(bend - b)
            blocks = []
            for i in range(nsub):
                rs = slice(i * SUB, (i + 1) * SUB)
                bi = b[rs]
                qi = q[rs]
                if i == 0:
                    sc = jnp.zeros((SUB, CHUNK), F32)
                else:
                    beta = b[i * SUB - 1:i * SUB, :]
                    qt = qi * jnp.exp(bi - beta)
                    rhs = jnp.where(rowc < i * SUB, khat * jnp.exp(jnp.minimum(beta - bend, 0.0)), 0.0)
                    sc = _dot_nt(_bf(qt), _bf(rhs))
                for s in range(SUB):
                    bs = bi[s:s + 1, :]
                    ks = k[i * SUB + s:i * SUB + s + 1, :]
                    ee = jnp.exp(jnp.minimum(bi - bs, 0.0))
                    col = jnp.sum(qi * ks * ee, axis=1, keepdims=True)
                    col = jnp.where(row_s >= s, col, 0.0)
                    sc = jnp.where(lane_c == i * SUB + s, col, sc)
                blocks.append(sc)
            scores = jnp.concatenate(blocks, axis=0)
            st = st_ref[hd]
            out = _dot(_bf(scores), v) + _dot_nt(_bf(q * jnp.exp(b)), _bf(st))
            b_end = b[CHUNK - 1:CHUNK, :]
            st_ref[hd] = st * jnp.exp(b_end) + _dot_tn(v, _bf(k * jnp.exp(b_end - b)))
            ms = jnp.mean(out * out, axis=1, keepdims=True)
            o_ref[0, rows, cs_] = (out * lax.rsqrt(ms + RMS_EPS) * og_s[rows, cs_]).astype(o_ref.dtype)
        return carry

    lax.fori_loop(0, tm // CHUNK, chunk, 0)


def _hgrn2(x, g, w_c, hg_lb, norm_g, layer, tm):
    B, S, D = x.shape
    W = HG_HEADS * HG_DK
    depth = hg_lb.shape[0]
    kern = functools.partial(_hg_kernel, tm=tm, layer=layer, depth=depth)
    return pl.pallas_call(
        kern,
        grid=(B, S // tm),
        in_specs=[
            pl.BlockSpec((1, tm, D), lambda b, i: (b, i, 0)),
            _resident((1, D)),
            _resident(w_c.shape),
            _resident(hg_lb.shape),
            _resident((1, W)),
        ],
        out_specs=pl.BlockSpec((1, tm, W), lambda b, i: (b, i, 0)),
        out_shape=jax.ShapeDtypeStruct((B, S, W), BF16),
        scratch_shapes=[pltpu.VMEM((tm, W), F32) for _ in range(5)]
        + [pltpu.VMEM((HG_HEADS, HG_DK, HG_DK), F32)],
        compiler_params=_params("parallel", "arbitrary"),
        name="hgrn2",
    )(x, g.reshape(1, D), w_c, hg_lb, norm_g.reshape(1, W))


RW_PASSES = 3


def _rw_chunk_head(r, lw, k, v, kk, bv, h0, tri, masks):
    strict, incl, same16, same32, eye = masks
    mm = functools.partial(_mm, passes=RW_PASSES)
    mm_nt = functools.partial(_mm, passes=RW_PASSES, dot=_dot_nt)
    mm_tn = functools.partial(_mm, passes=RW_PASSES, dot=_dot_tn)
    c = _mm_exact_lhs(tri, lw, 3)
    c_last = c[CHUNK - 1:CHUNK, :]
    e_neg = jnp.exp(-c)
    rt = r * jnp.exp(c)
    p = kk * jnp.exp(c - lw)
    kin = k * e_neg
    bin_ = bv * e_neg
    e_end = jnp.exp(c_last - c)
    kg = k * e_end
    bg = bv * e_end
    lhs = jnp.concatenate([p, rt], axis=0)
    ab = mm_nt(lhs, bin_)
    ak = mm_nt(lhs, kin)
    l_raw = ab[:CHUNK]
    m = jnp.where(strict, ak[:CHUNK], 0.0)
    arb = jnp.where(incl, ab[CHUNK:], 0.0)
    ark = jnp.where(incl, ak[CHUNK:], 0.0)
    n1 = jnp.where(strict & same16, -l_raw, 0.0)
    s1 = eye + n1
    n2 = mm(n1, n1)
    s2 = s1 + mm(s1, n2)
    n4 = mm(n2, n2)
    s4 = s2 + mm(s2, n4)
    n8 = mm(n4, n4)
    t16 = s4 + mm(s4, n8)
    lo1 = jnp.where(strict & same32 & jnp.logical_not(same16), l_raw, 0.0)
    t32 = t16 - mm(mm(t16, lo1), t16)
    lo2 = jnp.where(strict & jnp.logical_not(same32), l_raw, 0.0)
    t = t32 - mm(mm(t32, lo2), t32)
    mv = mm(m, v)
    pw = mm(t, p)
    uv = mm(t, mv)
    rq = rt - mm(arb, pw)
    yv = mm(ark, v) - mm(arb, uv)
    g = jnp.where(eye > 0.0, jnp.broadcast_to(jnp.exp(c_last), (RW_N, RW_N)), 0.0) - mm_tn(bg, pw)
    hv = mm_tn(kg, v) - mm_tn(bg, uv)
    y = mm(rq, h0) + yv
    h1 = mm(g, h0) + hv
    return y, h1


def _rw_kernel(*refs, tm, has_vres):
    if has_vres:
        (x_ref, g_ref, w_ref, mur_ref, mul_ref, w0_ref, w1_ref, w2_ref, a0_ref, a1_ref, a2_ref,
         g1_ref, g2_ref, kk_ref, ka_ref, rk_ref, lng_ref, lnb_ref, ones_ref,
         vf_ref, muv_ref, v0_ref, v1_ref, v2_ref,
         o_ref, r_s, lw_s, k_s, v_s, kk_s, bv_s, y_s, ch_s, cp_s, st_ref) = refs
    else:
        (x_ref, g_ref, w_ref, mur_ref, mul_ref, w0_ref, w1_ref, w2_ref, a0_ref, a1_ref, a2_ref,
         g1_ref, g2_ref, kk_ref, ka_ref, rk_ref, lng_ref, lnb_ref, ones_ref,
         o_ref, vf_out_ref, r_s, lw_s, k_s, v_s, kk_s, bv_s, y_s, ch_s, cp_s, st_ref) = refs
    W = RW_HEADS * RW_N

    @pl.when(pl.program_id(1) == 0)
    def _():
        st_ref[...] = jnp.zeros_like(st_ref)
        ch_s[...] = jnp.zeros_like(ch_s)
        cp_s[...] = jnp.zeros_like(cp_s)

    h = _rms(x_ref[0], g_ref[...])
    p = _dot(_bf(h), w_ref[...])
    row_d = _iota2(h.shape, 0)
    row_p = _iota2(p.shape, 0)
    hs = jnp.where(row_d == 0, ch_s[...], pltpu.roll(h, 1, 0))
    ps = jnp.where(row_p == 0, cp_s[...], pltpu.roll(p, 1, 0))
    ch_s[...] = h[tm - 1:tm, :]
    cp_s[...] = p[tm - 1:tm, :]
    dh = hs - h
    pm = p + (ps - p) * mur_ref[...]
    r = pm[:, 0:W]
    k = pm[:, W:2 * W]
    v = pm[:, 2 * W:3 * W]
    xw = _bf(h + dh * mul_ref[0:1, :])
    xa = _bf(h + dh * mul_ref[1:2, :])
    xg = _bf(h + dh * mul_ref[2:3, :])
    wl = w0_ref[...] + _dot(_bf(jnp.tanh(_dot(xw, w1_ref[...]))), w2_ref[...])
    w_log = -_softplus(-wl) - 0.5
    lw = -jnp.exp(w_log)
    if has_vres:
        xv = _bf(h + dh * muv_ref[...])
        mixv = _sigmoid(v0_ref[...] + _dot(_bf(_dot(xv, v1_ref[...])), v2_ref[...]))
        v = v + (vf_ref[0] - v) * mixv
    else:
        vf_out_ref[0] = v
    a = _sigmoid(a0_ref[...] + _dot(_bf(_dot(xa, a1_ref[...])), a2_ref[...]))
    gate = _dot(_bf(_sigmoid(_dot(xg, g1_ref[...]))), g2_ref[...])
    ones = ones_ref[...]
    kk = k * kk_ref[...]
    ssq = _mm_exact_rhs(kk * kk, ones, 3)
    kk = kk * lax.rsqrt(jnp.maximum(ssq, 1e-24))
    k = k * (1.0 + (a - 1.0) * ka_ref[...])
    r_s[...] = r
    lw_s[...] = lw
    k_s[...] = k
    v_s[...] = v
    kk_s[...] = kk
    bv_s[...] = kk * a

    ri = _iota2((CHUNK, CHUNK), 0)
    ci = _iota2((CHUNK, CHUNK), 1)
    strict = ci < ri
    incl = ci <= ri
    same16 = (ri // 16) == (ci // 16)
    same32 = (ri // 32) == (ci // 32)
    eye = (ri == ci).astype(F32)
    masks = (strict, incl, same16, same32, eye)
    tri = _bf(incl.astype(F32))

    def chunk(c, carry):
        rows = pl.ds(pl.multiple_of(c * CHUNK, CHUNK), CHUNK)
        for hd in range(RW_HEADS):
            cs_ = slice(hd * RW_N, (hd + 1) * RW_N)
            y, h1 = _rw_chunk_head(r_s[rows, cs_], lw_s[rows, cs_], k_s[rows, cs_], v_s[rows, cs_],
                                   kk_s[rows, cs_], bv_s[rows, cs_], st_ref[hd], tri, masks)
            st_ref[hd] = h1
            y_s[rows, cs_] = y
        return carry

    lax.fori_loop(0, tm // CHUNK, chunk, 0)

    y = y_s[...]
    inv_n = 1.0 / RW_N
    mean = _mm_exact_rhs(y, ones, 3) * inv_n
    yc = y - mean
    var = _mm_exact_rhs(yc * yc, ones, 3) * inv_n
    yn = yc * lax.rsqrt(var + RW_GN_EPS) * lng_ref[...] + lnb_ref[...]
    bonus = _mm_exact_rhs(r_s[...] * k_s[...] * rk_ref[...], ones, 3)
    yn = yn + bonus * v_s[...]
    o_ref[0] = (yn * gate).astype(o_ref.dtype)


def _rwkv7(x, g, w_d, mu_rkv, mu_lora, w0, w1, w2, a0, a1, a2, g1, g2, k_k, k_a, r_k, ln_g, ln_b,
           v_first, vres, tm):
    B, S, D = x.shape
    W = RW_HEADS * RW_N
    has_vres = vres is not None
    hid = jnp.arange(W) // RW_N
    ones = (hid[:, None] == hid[None, :]).astype(BF16)
    row = lambda t: t.reshape(1, -1)
    args = [x, row(g), w_d, row(mu_rkv), mu_lora, row(w0), _bf(w1), _bf(w2), row(a0), _bf(a1), _bf(a2),
            _bf(g1), _bf(g2), row(k_k), row(k_a), row(r_k), row(ln_g), row(ln_b), ones]
    in_specs = [pl.BlockSpec((1, tm, D), lambda b, i: (b, i, 0))] + [_resident(t.shape) for t in args[1:]]
    tile_w = pl.BlockSpec((1, tm, W), lambda b, i: (b, i, 0))
    if has_vres:
        mu_v, v0, v1, v2 = vres
        extra = [row(mu_v), row(v0), _bf(v1), _bf(v2)]
        args += [v_first] + extra
        in_specs += [tile_w] + [_resident(t.shape) for t in extra]
        out_specs = tile_w
        out_shape = jax.ShapeDtypeStruct((B, S, W), BF16)
    else:
        out_specs = [tile_w, tile_w]
        out_shape = [jax.ShapeDtypeStruct((B, S, W), BF16), jax.ShapeDtypeStruct((B, S, W), F32)]
    kern = functools.partial(_rw_kernel, tm=tm, has_vres=has_vres)
    res = pl.pallas_call(
        kern,
        grid=(B, S // tm),
        in_specs=in_specs,
        out_specs=out_specs,
        out_shape=out_shape,
        scratch_shapes=[pltpu.VMEM((tm, W), F32) for _ in range(7)]
        + [pltpu.VMEM((1, D), F32), pltpu.VMEM((1, 3 * W), F32), pltpu.VMEM((RW_HEADS, RW_N, RW_N), F32)],
        compiler_params=_params("parallel", "arbitrary"),
        name="rwkv7",
    )(*args)
    if has_vres:
        return res, v_first
    return res[0], res[1]


def _merge_kernel(x_ref, g_ref, ya_ref, yb_ref, yc_ref, yd_ref, wg_ref, bg_ref, wb_ref, wo_ref, o_ref):
    x = x_ref[...]
    D = x.shape[1]
    h = _bf(_rms(x, g_ref[...]))
    mixed = None
    for j, y_ref in enumerate((ya_ref, yb_ref, yc_ref, yd_ref)):
        gate = _sigmoid(_dot(h, wg_ref[:, j * D:(j + 1) * D]) + bg_ref[:, j * D:(j + 1) * D])
        t = gate * _dot(y_ref[...], wb_ref[j])
        mixed = t if mixed is None else mixed + t
    o_ref[...] = x + _dot(_bf(mixed), wo_ref[...])


def _merge(x2, g, ys, w_gate, b_gate, w_branch, w_out, tm):
    T, D = x2.shape
    tile = lambda w: pl.BlockSpec((tm, w), lambda i: (i, 0))
    return pl.pallas_call(
        _merge_kernel,
        grid=(T // tm,),
        in_specs=[tile(D), _resident((1, D))] + [tile(BRANCH_W)] * 4
        + [_resident(w_gate.shape), _resident(b_gate.shape), _resident(w_branch.shape), _resident(w_out.shape)],
        out_specs=tile(D),
        out_shape=jax.ShapeDtypeStruct((T, D), F32),
        compiler_params=_params("parallel"),
        name="merge",
    )(x2, g.reshape(1, D), *ys, w_gate, b_gate, w_branch, w_out)


def _ffn_kernel(x_ref, g_ref, wg_ref, wu_ref, wd_ref, fg_ref, o_ref, *, hc, final):
    x = x_ref[...]
    h = _bf(_rms(x, g_ref[...]))
    hidden = wg_ref.shape[1]
    acc = x
    for c in range(hidden // hc):
        cs = slice(c * hc, (c + 1) * hc)
        act = _silu(_dot(h, wg_ref[:, cs])) * _dot(h, wu_ref[:, cs])
        acc = acc + _dot(_bf(act), wd_ref[cs, :])
    if final:
        acc = _rms(acc, fg_ref[...])
    o_ref[...] = acc


def _ffn(x2, g, w_gate, w_up, w_down, final_g, final, tm):
    T, D = x2.shape
    hidden = w_gate.shape[1]
    kern = functools.partial(_ffn_kernel, hc=256, final=final)
    return pl.pallas_call(
        kern,
        grid=(T // tm,),
        in_specs=[pl.BlockSpec((tm, D), lambda i: (i, 0)), _resident((1, D)),
                  _resident(w_gate.shape), _resident(w_up.shape), _resident(w_down.shape), _resident((1, D))],
        out_specs=pl.BlockSpec((tm, D), lambda i: (i, 0)),
        out_shape=jax.ShapeDtypeStruct((T, D), F32),
        compiler_params=_params("parallel"),
        name="ffn",
    )(x2, g.reshape(1, D), w_gate, w_up, w_down, final_g.reshape(1, D))


def kernel(x, norm_mix_g, w_in, w_gate, b_gate, conv_w, conv_b, conv_ln_g, conv_ln_b, sgu_ln_g, sgu_ln_b, sgu_w, sgu_b, hg_lb, hg_norm_g, rw_mu_rkv, rw_mu_lora, rw_w0, rw_w1, rw_w2, rw_a0, rw_a1, rw_a2, rw_g1, rw_g2, rw_k_k, rw_k_a, rw_r_k, rw_ln_g, rw_ln_b, rw_mu_vres, rw_v0, rw_v1, rw_v2, w_branch, w_out, norm_ffn_g, w_ffn_gate, w_ffn_up, w_ffn_down, final_norm_g):
    B, S, D = x.shape
    depth = w_in.shape[0]
    T = B * S
    tm = 256
    n_ab = 4 * BRANCH_W
    n_c = n_ab + 4 * HG_HEADS * HG_DK
    v_first = None
    for l in range(depth):
        w_in_l = _bf(w_in[l])
        x2 = x.reshape(T, D)
        a, y_b = _proj_ab(x2, norm_mix_g[l], w_in_l[:, :n_ab], sgu_ln_g[l], sgu_ln_b[l], sgu_w[l], sgu_b[l], tm)
        y_a = _conv(a, conv_w[l], conv_b[l], conv_ln_g[l], conv_ln_b[l], S, tm)
        y_c = _hgrn2(x, norm_mix_g[l], w_in_l[:, n_ab:n_c], hg_lb, hg_norm_g[l], l, tm)
        vres = None if l == 0 else (rw_mu_vres[l - 1], rw_v0[l - 1], rw_v1[l - 1], rw_v2[l - 1])
        y_d, v_first = _rwkv7(x, norm_mix_g[l], w_in_l[:, n_c:], rw_mu_rkv[l], rw_mu_lora[l],
                              rw_w0[l], rw_w1[l], rw_w2[l], rw_a0[l], rw_a1[l], rw_a2[l],
                              rw_g1[l], rw_g2[l], rw_k_k[l], rw_k_a[l], rw_r_k[l], rw_ln_g[l], rw_ln_b[l],
                              v_first, vres, tm)
        wg = _bf(jnp.concatenate([w_gate[l, j] for j in range(4)], axis=1))
        bg = b_gate[l].reshape(1, -1)
        x2 = _merge(x2, norm_mix_g[l], (y_a, y_b, y_c.reshape(T, -1), y_d.reshape(T, -1)),
                    wg, bg, _bf(w_branch[l]), _bf(w_out[l]), tm)
        x2 = _ffn(x2, norm_ffn_g[l], _bf(w_ffn_gate[l]), _bf(w_ffn_up[l]), _bf(w_ffn_down[l]),
                  final_norm_g, l == depth - 1, tm)
        x = x2.reshape(B, S, D)
    return x
```

```python
import functools

import jax
import jax.numpy as jnp
from jax import lax
from jax.experimental import pallas as pl
from jax.experimental.pallas import tpu as pltpu

F32 = jnp.float32
BF16 = jnp.bfloat16

CHUNK = 64
SUB = 16
BRANCH_W = 512
CONV_WIDTH = 31
CONV_HALO = 32
SGU_GROUPS = 4
SGU_WIN = 128
HG_HEADS = 4
HG_DK = 128
RW_HEADS = 8
RW_N = 64
RMS_EPS = 1e-6
LN_EPS = 1e-5
RW_GN_EPS = 64e-5
VMEM_LIMIT = 56 * 1024 * 1024


def _bf(x):
    return x.astype(BF16)


def _dot(a, b):
    return jnp.dot(a, b, preferred_element_type=F32)


def _dot_nt(a, b):
    return lax.dot_general(a, b, (((1,), (1,)), ((), ())), preferred_element_type=F32)


def _dot_tn(a, b):
    return lax.dot_general(a, b, (((0,), (0,)), ((), ())), preferred_element_type=F32)


def _split(x, n):
    parts = []
    r = x
    for _ in range(n - 1):
        p = _bf(r)
        parts.append(p)
        r = r - p.astype(F32)
    parts.append(_bf(r))
    return parts


def _mm(a, b, passes, dot=_dot):
    if passes == 1:
        return dot(_bf(a), _bf(b))
    n = 2 if passes == 3 else 3
    ap, bp = _split(a, n), _split(b, n)
    acc = None
    for i in range(n):
        for j in range(n):
            if i + j < n:
                t = dot(ap[i], bp[j])
                acc = t if acc is None else acc + t
    return acc


def _mm_exact_lhs(a_bf, b, n, dot=_dot):
    acc = None
    for p in _split(b, n):
        t = dot(a_bf, p)
        acc = t if acc is None else acc + t
    return acc


def _mm_exact_rhs(a, b_bf, n):
    acc = None
    for p in _split(a, n):
        t = _dot(p, b_bf)
        acc = t if acc is None else acc + t
    return acc


def _rms(x, g, eps=RMS_EPS):
    return x * lax.rsqrt(jnp.mean(x * x, axis=-1, keepdims=True) + eps) * g


def _layer_norm(x, g, b, eps=LN_EPS):
    xc = x - jnp.mean(x, axis=-1, keepdims=True)
    return xc * lax.rsqrt(jnp.mean(xc * xc, axis=-1, keepdims=True) + eps) * g + b


def _sigmoid(x):
    return 1.0 / (1.0 + jnp.exp(-x))


def _silu(x):
    return x * _sigmoid(x)


def _gelu(x):
    return 0.5 * x * (1.0 + lax.erf(x * 0.7071067811865476))


def _softplus(x):
    return jnp.maximum(x, 0.0) + jnp.log1p(jnp.exp(-jnp.abs(x)))


def _iota2(shape, dim):
    return lax.broadcasted_iota(jnp.int32, shape, dim)


def _resident(shape):
    nd = len(shape)
    return pl.BlockSpec(shape, lambda *_: (0,) * nd, pipeline_mode=pl.Buffered(1))


def _params(*sem):
    return pltpu.CompilerParams(dimension_semantics=sem, vmem_limit_bytes=VMEM_LIMIT)


def _ab_kernel(x_ref, g_ref, w_ref, lng_ref, lnb_ref, wsp_ref, bsp_ref, a_ref, yb_ref, *, tm):
    h = _bf(_rms(x_ref[...], g_ref[...]))
    pa = _dot(h, w_ref[:, 0:2 * BRANCH_W])
    a_ref[...] = pa[:, :BRANCH_W] * _sigmoid(pa[:, BRANCH_W:])
    pb = _dot(h, w_ref[:, 2 * BRANCH_W:4 * BRANCH_W])
    u = _gelu(pb[:, :BRANCH_W])
    v = _bf(_layer_norm(_gelu(pb[:, BRANCH_W:]), lng_ref[...], lnb_ref[...]))
    row = _iota2((SGU_WIN, SGU_WIN), 0)
    col = _iota2((SGU_WIN, SGU_WIN), 1)
    allowed = (col // CHUNK) <= (row // CHUNK)
    gw = BRANCH_W // SGU_GROUPS
    for g in range(SGU_GROUPS):
        wg = _bf(jnp.where(allowed, wsp_ref[g], 0.0))
        bg = bsp_ref[g]
        for w in range(tm // SGU_WIN):
            rs = slice(w * SGU_WIN, (w + 1) * SGU_WIN)
            cs = slice(g * gw, (g + 1) * gw)
            mixed = _dot(wg, v[rs, cs]) + bg
            yb_ref[rs, cs] = (u[rs, cs] * mixed).astype(yb_ref.dtype)


def _proj_ab(x2, g, w_ab, sgu_ln_g, sgu_ln_b, sgu_w, sgu_b, tm):
    T, D = x2.shape
    kern = functools.partial(_ab_kernel, tm=tm)
    return pl.pallas_call(
        kern,
        grid=(T // tm,),
        in_specs=[
            pl.BlockSpec((tm, D), lambda i: (i, 0)),
            _resident((1, D)),
            _resident(w_ab.shape),
            _resident((1, BRANCH_W)),
            _resident((1, BRANCH_W)),
            _resident(sgu_w.shape),
            _resident((SGU_GROUPS, SGU_WIN, 1)),
        ],
        out_specs=[
            pl.BlockSpec((tm, BRANCH_W), lambda i: (i, 0)),
            pl.BlockSpec((tm, BRANCH_W), lambda i: (i, 0)),
        ],
        out_shape=[
            jax.ShapeDtypeStruct((T, BRANCH_W), F32),
            jax.ShapeDtypeStruct((T, BRANCH_W), BF16),
        ],
        compiler_params=_params("parallel"),
        name="proj_ab",
    )(x2, g.reshape(1, D), w_ab, sgu_ln_g.reshape(1, -1), sgu_ln_b.reshape(1, -1), sgu_w,
      sgu_b.reshape(SGU_GROUPS, SGU_WIN, 1))


def _conv_kernel(a_ref, halo_ref, w_ref, b_ref, lng_ref, lnb_ref, o_ref, buf_ref, *, tm, seq, rb):
    first = (pl.program_id(0) * tm) % seq == 0
    buf_ref[0:CONV_HALO, :] = jnp.where(first, 0.0, halo_ref[...])
    buf_ref[CONV_HALO:, :] = a_ref[...]
    off = CONV_HALO - (CONV_WIDTH - 1)
    for r in range(tm // rb):
        acc = jnp.zeros((rb, BRANCH_W), F32) + b_ref[...]
        for j in range(CONV_WIDTH):
            acc = acc + w_ref[j:j + 1, :] * buf_ref[pl.ds(r * rb + off + j, rb), :]
        y = _layer_norm(acc, lng_ref[...], lnb_ref[...])
        o_ref[r * rb:(r + 1) * rb, :] = _silu(y).astype(o_ref.dtype)


def _conv(a, conv_w, conv_b, ln_g, ln_b, seq, tm):
    T, C = a.shape
    kern = functools.partial(_conv_kernel, tm=tm, seq=seq, rb=64)
    hb = tm // CONV_HALO
    return pl.pallas_call(
        kern,
        grid=(T // tm,),
        in_specs=[
            pl.BlockSpec((tm, C), lambda i: (i, 0)),
            pl.BlockSpec((CONV_HALO, C), lambda i: (jnp.maximum(i * hb - 1, 0), 0)),
            _resident(conv_w.shape),
            _resident((1, C)),
            _resident((1, C)),
            _resident((1, C)),
        ],
        out_specs=pl.BlockSpec((tm, C), lambda i: (i, 0)),
        out_shape=jax.ShapeDtypeStruct((T, C), BF16),
        scratch_shapes=[pltpu.VMEM((tm + CONV_HALO, C), F32)],
        compiler_params=_params("parallel"),
        name="conv",
    )(a, a, conv_w, conv_b.reshape(1, C), ln_g.reshape(1, C), ln_b.reshape(1, C))


def _hg_kernel(x_ref, g_ref, w_ref, lb_ref, ng_ref, o_ref,
               q_s, lf_s, k_s, v_s, og_s, st_ref, *, tm, layer, depth):
    @pl.when(pl.program_id(1) == 0)
    def _():
        st_ref[...] = jnp.zeros_like(st_ref)

    W = HG_HEADS * HG_DK
    h = _bf(_rms(x_ref[0], g_ref[...]))
    p = _dot(h, w_ref[...])
    z = p[:, W:2 * W]
    lbp = lb_ref[...]
    e = jnp.exp(lbp - jnp.max(lbp, axis=0, keepdims=True))
    sm = e / jnp.sum(e, axis=0, keepdims=True)
    cs0 = sm[0:1]
    cs = cs0
    for l in range(1, layer + 1):
        cs = cs + sm[l:l + 1]
    lb = cs - cs0
    log_sig = -_softplus(-z)
    t1 = jnp.log(lb)
    t2 = jnp.log1p(-lb) + log_sig
    log_f = jnp.maximum(t1, t2) + jnp.log1p(jnp.exp(-jnp.abs(t1 - t2)))
    q_s[...] = _silu(p[:, 0:W])
    lf_s[...] = log_f
    k_s[...] = (1.0 - lb) * _sigmoid(-z)
    v_s[...] = p[:, 2 * W:3 * W]
    og_s[...] = _silu(p[:, 3 * W:4 * W]) * ng_ref[...]

    tri = _bf((_iota2((CHUNK, CHUNK), 1) <= _iota2((CHUNK, CHUNK), 0)).astype(F32))
    nsub = CHUNK // SUB
    rowc = _iota2((CHUNK, HG_DK), 0)
    row_s = _iota2((SUB, 1), 0)
    lane_c = _iota2((SUB, CHUNK), 1)

    def chunk(c, carry):
        r0 = pl.multiple_of(c * CHUNK, CHUNK)
        rows = pl.ds(r0, CHUNK)
        for hd in range(HG_HEADS):
            cs_ = slice(hd * HG_DK, (hd + 1) * HG_DK)
            q = q_s[rows, cs_]
            lf = lf_s[rows, cs_]
            k = k_s[rows, cs_]
            v = _bf(v_s[rows, cs_])
            b = _mm_exact_lhs(tri, lf, 3)
            bend = jnp.concatenate(
                [jnp.broadcast_to(b[(j + 1) * SUB - 1:(j + 1) * SUB, :], (SUB, HG_DK)) for j in range(nsub)], axis=0)
            khat = k * jnp.exp(bend - b)
            blocks = []
            for i in range(nsub):
                rs = slice(i * SUB, (i + 1) * SUB)
                bi = b[rs]
                qi = q[rs]
                if i == 0:
                    sc = jnp.zeros((SUB, CHUNK), F32)
                else:
                    beta = b[i * SUB - 1:i * SUB, :]
                    qt = qi * jnp.exp(bi - beta)
                    rhs = jnp.where(rowc < i * SUB, khat * jnp.exp(jnp.minimum(beta - bend, 0.0)), 0.0)
                    sc = _dot_nt(_bf(qt), _bf(rhs))
                for s in range(SUB):
                    bs = bi[s:s + 1, :]
                    ks = k[i * SUB + s:i * SUB + s + 1, :]
                    ee = jnp.exp(jnp.minimum(bi - bs, 0.0))
                    col = jnp.sum(qi * ks * ee, axis=1, keepdims=True)
                    col = jnp.where(row_s >= s, col, 0.0)
                    sc = jnp.where(lane_c == i * SUB + s, col, sc)
                blocks.append(sc)
            scores = jnp.concatenate(blocks, axis=0)
            st = st_ref[hd]
            out = _dot(_bf(scores), v) + _dot_nt(_bf(q * jnp.exp(b)), _bf(st))
            b_end = b[CHUNK - 1:CHUNK, :]
            st_ref[hd] = st * jnp.exp(b_end) + _dot_tn(v, _bf(k * jnp.exp(b_end - b)))
            ms = jnp.mean(out * out, axis=1, keepdims=True)
            o_ref[0, rows, cs_] = (out * lax.rsqrt(ms + RMS_EPS) * og_s[rows, cs_]).astype(o_ref.dtype)
        return carry

    lax.fori_loop(0, tm // CHUNK, chunk, 0)


def _hgrn2(x, g, w_c, hg_lb, norm_g, layer, tm):
    B, S, D = x.shape
    W = HG_HEADS * HG_DK
    depth = hg_lb.shape[0]
    kern = functools.partial(_hg_kernel, tm=tm, layer=layer, depth=depth)
    return pl.pallas_call(
        kern,
        grid=(B, S // tm),
        in_specs=[
            pl.BlockSpec((1, tm, D), lambda b, i: (b, i, 0)),
            _resident((1, D)),
            _resident(w_c.shape),
            _resident(hg_lb.shape),
            _resident((1, W)),
        ],
        out_specs=pl.BlockSpec((1, tm, W), lambda b, i: (b, i, 0)),
        out_shape=jax.ShapeDtypeStruct((B, S, W), BF16),
        scratch_shapes=[pltpu.VMEM((tm, W), F32) for _ in range(5)]
        + [pltpu.VMEM((HG_HEADS, HG_DK, HG_DK), F32)],
        compiler_params=_params("parallel", "arbitrary"),
        name="hgrn2",
    )(x, g.reshape(1, D), w_c, hg_lb, norm_g.reshape(1, W))


RW_PASSES = 1


def _rw_chunk(r, lw, k, v, kk, bv, h0s, tri, masks):
    strict, incl, same16, same32, eye = masks
    mm = functools.partial(_mm, passes=RW_PASSES)
    mm_nt = functools.partial(_mm, passes=RW_PASSES, dot=_dot_nt)
    mm_tn = functools.partial(_mm, passes=RW_PASSES, dot=_dot_tn)
    H = range(RW_HEADS)
    sl = lambda a, h: a[:, h * RW_N:(h + 1) * RW_N]
    each = lambda f: [f(h) for h in H]
    c = _mm_exact_lhs(tri, lw, 3)
    c_last = c[CHUNK - 1:CHUNK, :]
    e_neg = jnp.exp(-c)
    rt = r * jnp.exp(c)
    p = kk * jnp.exp(c - lw)
    kin = k * e_neg
    bin_ = bv * e_neg
    e_end = jnp.exp(c_last - c)
    kg = k * e_end
    bg = bv * e_end
    g_diag = jnp.exp(c_last)
    lhs = each(lambda h: jnp.concatenate([sl(p, h), sl(rt, h)], axis=0))
    ab = each(lambda h: mm_nt(lhs[h], sl(bin_, h)))
    ak = each(lambda h: mm_nt(lhs[h], sl(kin, h)))
    l_raw = each(lambda h: ab[h][:CHUNK])
    m = each(lambda h: jnp.where(strict, ak[h][:CHUNK], 0.0))
    arb = each(lambda h: jnp.where(incl, ab[h][CHUNK:], 0.0))
    ark = each(lambda h: jnp.where(incl, ak[h][CHUNK:], 0.0))
    n1 = each(lambda h: jnp.where(strict & same16, -l_raw[h], 0.0))
    n2 = each(lambda h: mm(n1[h], n1[h]))
    s2 = each(lambda h: eye + n1[h] + n2[h] + mm(n1[h], n2[h]))
    n4 = each(lambda h: mm(n2[h], n2[h]))
    s4 = each(lambda h: s2[h] + mm(s2[h], n4[h]))
    n8 = each(lambda h: mm(n4[h], n4[h]))
    t16 = each(lambda h: s4[h] + mm(s4[h], n8[h]))
    lo1 = each(lambda h: jnp.where(strict & same32 & jnp.logical_not(same16), l_raw[h], 0.0))
    x1 = each(lambda h: mm(t16[h], lo1[h]))
    t32 = each(lambda h: t16[h] - mm(x1[h], t16[h]))
    lo2 = each(lambda h: jnp.where(strict & jnp.logical_not(same32), l_raw[h], 0.0))
    x2 = each(lambda h: mm(t32[h], lo2[h]))
    t = each(lambda h: t32[h] - mm(x2[h], t32[h]))
    mv = each(lambda h: mm(m[h], sl(v, h)))
    pw = each(lambda h: mm(t[h], sl(p, h)))
    uv = each(lambda h: mm(t[h], mv[h]))
    rq = each(lambda h: sl(rt, h) - mm(arb[h], pw[h]))
    yv = each(lambda h: mm(ark[h], sl(v, h)) - mm(arb[h], uv[h]))
    g = each(lambda h: jnp.where(eye > 0.0, jnp.broadcast_to(sl(g_diag, h), (RW_N, RW_N)), 0.0)
             - mm_tn(sl(bg, h), pw[h]))
    hv = each(lambda h: mm_tn(sl(kg, h), sl(v, h)) - mm_tn(sl(bg, h), uv[h]))
    y = each(lambda h: mm(rq[h], h0s[h]) + yv[h])
    h1 = each(lambda h: mm(g[h], h0s[h]) + hv[h])
    return y, h1


def _rw_kernel(*refs, tm, has_vres):
    if has_vres:
        (x_ref, g_ref, w_ref, mur_ref, mul_ref, w0_ref, w1_ref, w2_ref, a0_ref, a1_ref, a2_ref,
         g1_ref, g2_ref, kk_ref, ka_ref, rk_ref, lng_ref, lnb_ref, ones_ref,
         vf_ref, muv_ref, v0_ref, v1_ref, v2_ref,
         o_ref, r_s, lw_s, k_s, v_s, kk_s, bv_s, y_s, ch_s, cp_s, st_ref) = refs
    else:
        (x_ref, g_ref, w_ref, mur_ref, mul_ref, w0_ref, w1_ref, w2_ref, a0_ref, a1_ref, a2_ref,
         g1_ref, g2_ref, kk_ref, ka_ref, rk_ref, lng_ref, lnb_ref, ones_ref,
         o_ref, vf_out_ref, r_s, lw_s, k_s, v_s, kk_s, bv_s, y_s, ch_s, cp_s, st_ref) = refs
    W = RW_HEADS * RW_N

    @pl.when(pl.program_id(1) == 0)
    def _():
        st_ref[...] = jnp.zeros_like(st_ref)
        ch_s[...] = jnp.zeros_like(ch_s)
        cp_s[...] = jnp.zeros_like(cp_s)

    h = _rms(x_ref[0], g_ref[...])
    p = _dot(_bf(h), w_ref[...])
    row_d = _iota2(h.shape, 0)
    row_p = _iota2(p.shape, 0)
    hs = jnp.where(row_d == 0, ch_s[...], pltpu.roll(h, 1, 0))
    ps = jnp.where(row_p == 0, cp_s[...], pltpu.roll(p, 1, 0))
    ch_s[...] = h[tm - 1:tm, :]
    cp_s[...] = p[tm - 1:tm, :]
    dh = hs - h
    pm = p + (ps - p) * mur_ref[...]
    r = pm[:, 0:W]
    k = pm[:, W:2 * W]
    v = pm[:, 2 * W:3 * W]
    xw = _bf(h + dh * mul_ref[0:1, :])
    xa = _bf(h + dh * mul_ref[1:2, :])
    xg = _bf(h + dh * mul_ref[2:3, :])
    wl = w0_ref[...] + _dot(_bf(jnp.tanh(_dot(xw, w1_ref[...]))), w2_ref[...])
    w_log = -_softplus(-wl) - 0.5
    lw = -jnp.exp(w_log)
    if has_vres:
        xv = _bf(h + dh * muv_ref[...])
        mixv = _sigmoid(v0_ref[...] + _dot(_bf(_dot(xv, v1_ref[...])), v2_ref[...]))
        v = v + (vf_ref[0] - v) * mixv
    else:
        vf_out_ref[0] = v
    a = _sigmoid(a0_ref[...] + _dot(_bf(_dot(xa, a1_ref[...])), a2_ref[...]))
    gate = _dot(_bf(_sigmoid(_dot(xg, g1_ref[...]))), g2_ref[...])
    ones = ones_ref[...]
    kk = k * kk_ref[...]
    ssq = _mm_exact_rhs(kk * kk, ones, 1)
    kk = kk * lax.rsqrt(jnp.maximum(ssq, 1e-24))
    k = k * (1.0 + (a - 1.0) * ka_ref[...])
    r_s[...] = r
    lw_s[...] = lw
    k_s[...] = k
    v_s[...] = v
    kk_s[...] = kk
    bv_s[...] = kk * a

    ri = _iota2((CHUNK, CHUNK), 0)
    ci = _iota2((CHUNK, CHUNK), 1)
    strict = ci < ri
    incl = ci <= ri
    same16 = (ri // 16) == (ci // 16)
    same32 = (ri // 32) == (ci // 32)
    eye = (ri == ci).astype(F32)
    masks = (strict, incl, same16, same32, eye)
    tri = _bf(incl.astype(F32))

    def chunk(c, carry):
        rows = pl.ds(pl.multiple_of(c * CHUNK, CHUNK), CHUNK)
        y, h1 = _rw_chunk(r_s[rows, :], lw_s[rows, :], k_s[rows, :], v_s[rows, :], kk_s[rows, :], bv_s[rows, :],
                          [st_ref[hd] for hd in range(RW_HEADS)], tri, masks)
        for hd in range(RW_HEADS):
            st_ref[hd] = h1[hd]
        y_s[rows, :] = jnp.concatenate(y, axis=1)
        return carry

    lax.fori_loop(0, tm // CHUNK, chunk, 0)

    y = y_s[...]
    inv_n = 1.0 / RW_N
    mean = _mm_exact_rhs(y, ones, 2) * inv_n
    yc = y - mean
    var = _mm_exact_rhs(yc * yc, ones, 1) * inv_n
    yn = yc * lax.rsqrt(var + RW_GN_EPS) * lng_ref[...] + lnb_ref[...]
    bonus = _mm_exact_rhs(r_s[...] * k_s[...] * rk_ref[...], ones, 1)
    yn = yn + bonus * v_s[...]
    o_ref[0] = (yn * gate).astype(o_ref.dtype)


def _rwkv7(x, g, w_d, mu_rkv, mu_lora, w0, w1, w2, a0, a1, a2, g1, g2, k_k, k_a, r_k, ln_g, ln_b,
           v_first, vres, tm):
    B, S, D = x.shape
    W = RW_HEADS * RW_N
    has_vres = vres is not None
    hid = jnp.arange(W) // RW_N
    ones = (hid[:, None] == hid[None, :]).astype(BF16)
    row = lambda t: t.reshape(1, -1)
    args = [x, row(g), w_d, row(mu_rkv), mu_lora, row(w0), _bf(w1), _bf(w2), row(a0), _bf(a1), _bf(a2),
            _bf(g1), _bf(g2), row(k_k), row(k_a), row(r_k), row(ln_g), row(ln_b), ones]
    in_specs = [pl.BlockSpec((1, tm, D), lambda b, i: (b, i, 0))] + [_resident(t.shape) for t in args[1:]]
    tile_w = pl.BlockSpec((1, tm, W), lambda b, i: (b, i, 0))
    if has_vres:
        mu_v, v0, v1, v2 = vres
        extra = [row(mu_v), row(v0), _bf(v1), _bf(v2)]
        args += [v_first] + extra
        in_specs += [tile_w] + [_resident(t.shape) for t in extra]
        out_specs = tile_w
        out_shape = jax.ShapeDtypeStruct((B, S, W), BF16)
    else:
        out_specs = [tile_w, tile_w]
        out_shape = [jax.ShapeDtypeStruct((B, S, W), BF16), jax.ShapeDtypeStruct((B, S, W), F32)]
    kern = functools.partial(_rw_kernel, tm=tm, has_vres=has_vres)
    res = pl.pallas_call(
        kern,
        grid=(B, S // tm),
        in_specs=in_specs,
        out_specs=out_specs,
        out_shape=out_shape,
        scratch_shapes=[pltpu.VMEM((tm, W), F32) for _ in range(7)]
        + [pltpu.VMEM((1, D), F32), pltpu.VMEM((1, 3 * W), F32), pltpu.VMEM((RW_HEADS, RW_N, RW_N), F32)],
        compiler_params=_params("parallel", "arbitrary"),
        name="rwkv7",
    )(*args)
    if has_vres:
        return res, v_first
    return res[0], res[1]


def _merge_kernel(x_ref, g_ref, ya_ref, yb_ref, yc_ref, yd_ref, wg_ref, bg_ref, wb_ref, wo_ref, o_ref):
    x = x_ref[...]
    D = x.shape[1]
    h = _bf(_rms(x, g_ref[...]))
    mixed = None
    for j, y_ref in enumerate((ya_ref, yb_ref, yc_ref, yd_ref)):
        gate = _sigmoid(_dot(h, wg_ref[:, j * D:(j + 1) * D]) + bg_ref[:, j * D:(j + 1) * D])
        t = gate * _dot(y_ref[...], wb_ref[j])
        mixed = t if mixed is None else mixed + t
    o_ref[...] = x + _dot(_bf(mixed), wo_ref[...])


def _merge(x2, g, ys, w_gate, b_gate, w_branch, w_out, tm):
    T, D = x2.shape
    tile = lambda w: pl.BlockSpec((tm, w), lambda i: (i, 0))
    return pl.pallas_call(
        _merge_kernel,
        grid=(T // tm,),
        in_specs=[tile(D), _resident((1, D))] + [tile(BRANCH_W)] * 4
        + [_resident(w_gate.shape), _resident(b_gate.shape), _resident(w_branch.shape), _resident(w_out.shape)],
        out_specs=tile(D),
        out_shape=jax.ShapeDtypeStruct((T, D), F32),
        compiler_params=_params("parallel"),
        name="merge",
    )(x2, g.reshape(1, D), *ys, w_gate, b_gate, w_branch, w_out)


def _ffn_kernel(x_ref, g_ref, wg_ref, wu_ref, wd_ref, fg_ref, o_ref, *, hc, final):
    x = x_ref[...]
    h = _bf(_rms(x, g_ref[...]))
    hidden = wg_ref.shape[1]
    acc = x
    for c in range(hidden // hc):
        cs = slice(c * hc, (c + 1) * hc)
        act = _silu(_dot(h, wg_ref[:, cs])) * _dot(h, wu_ref[:, cs])
        acc = acc + _dot(_bf(act), wd_ref[cs, :])
    if final:
        acc = _rms(acc, fg_ref[...])
    o_ref[...] = acc


def _ffn(x2, g, w_gate, w_up, w_down, final_g, final, tm):
    T, D = x2.shape
    hidden = w_gate.shape[1]
    kern = functools.partial(_ffn_kernel, hc=256, final=final)
    return pl.pallas_call(
        kern,
        grid=(T // tm,),
        in_specs=[pl.BlockSpec((tm, D), lambda i: (i, 0)), _resident((1, D)),
                  _resident(w_gate.shape), _resident(w_up.shape), _resident(w_down.shape), _resident((1, D))],
        out_specs=pl.BlockSpec((tm, D), lambda i: (i, 0)),
        out_shape=jax.ShapeDtypeStruct((T, D), F32),
        compiler_params=_params("parallel"),
        name="ffn",
    )(x2, g.reshape(1, D), w_gate, w_up, w_down, final_g.reshape(1, D))


def kernel(x, norm_mix_g, w_in, w_gate, b_gate, conv_w, conv_b, conv_ln_g, conv_ln_b, sgu_ln_g, sgu_ln_b, sgu_w, sgu_b, hg_lb, hg_norm_g, rw_mu_rkv, rw_mu_lora, rw_w0, rw_w1, rw_w2, rw_a0, rw_a1, rw_a2, rw_g1, rw_g2, rw_k_k, rw_k_a, rw_r_k, rw_ln_g, rw_ln_b, rw_mu_vres, rw_v0, rw_v1, rw_v2, w_branch, w_out, norm_ffn_g, w_ffn_gate, w_ffn_up, w_ffn_down, final_norm_g):
    B, S, D = x.shape
    depth = w_in.shape[0]
    T = B * S
    tm = 256
    tm_mm = 512
    n_ab = 4 * BRANCH_W
    n_c = n_ab + 4 * HG_HEADS * HG_DK
    v_first = None
    for l in range(depth):
        w_in_l = _bf(w_in[l])
        x2 = x.reshape(T, D)
        a, y_b = _proj_ab(x2, norm_mix_g[l], w_in_l[:, :n_ab], sgu_ln_g[l], sgu_ln_b[l], sgu_w[l], sgu_b[l], tm_mm)
        y_a = _conv(a, conv_w[l], conv_b[l], conv_ln_g[l], conv_ln_b[l], S, tm)
        y_c = _hgrn2(x, norm_mix_g[l], w_in_l[:, n_ab:n_c], hg_lb, hg_norm_g[l], l, tm)
        vres = None if l == 0 else (rw_mu_vres[l - 1], rw_v0[l - 1], rw_v1[l - 1], rw_v2[l - 1])
        y_d, v_first = _rwkv7(x, norm_mix_g[l], w_in_l[:, n_c:], rw_mu_rkv[l], rw_mu_lora[l],
                              rw_w0[l], rw_w1[l], rw_w2[l], rw_a0[l], rw_a1[l], rw_a2[l],
                              rw_g1[l], rw_g2[l], rw_k_k[l], rw_k_a[l], rw_r_k[l], rw_ln_g[l], rw_ln_b[l],
                              v_first, vres, tm)
        wg = _bf(jnp.concatenate([w_gate[l, j] for j in range(4)], axis=1))
        bg = b_gate[l].reshape(1, -1)
        x2 = _merge(x2, norm_mix_g[l], (y_a, y_b, y_c.reshape(T, -1), y_d.reshape(T, -1)),
                    wg, bg, _bf(w_branch[l]), _bf(w_out[l]), tm_mm)
        x2 = _ffn(x2, norm_ffn_g[l], _bf(w_ffn_gate[l]), _bf(w_ffn_up[l]), _bf(w_ffn_down[l]),
                  final_norm_g, l == depth - 1, tm_mm)
        x = x2.reshape(B, S, D)
    return x
```

```python
import functools

import jax
import jax.numpy as jnp
from jax import lax
from jax.experimental import pallas as pl
from jax.experimental.pallas import tpu as pltpu

F32 = jnp.float32
BF16 = jnp.bfloat16

CHUNK = 64
SUB = 8
BRANCH_W = 512
CONV_WIDTH = 31
SUBLANES = 8
CONV_HALO = 32
SGU_GROUPS = 4
SGU_WIN = 128
HG_HEADS = 4
HG_UNROLL = 4
HG_DK = 128
RW_HEADS = 8
RW_N = 64
RMS_EPS = 1e-6
LN_EPS = 1e-5
RW_GN_EPS = 64e-5
VMEM_LIMIT = 56 * 1024 * 1024


def _bf(x):
    return x.astype(BF16)


def _dot(a, b):
    return jnp.dot(a, b, preferred_element_type=F32)


def _dot_nt(a, b):
    return lax.dot_general(a, b, (((1,), (1,)), ((), ())), preferred_element_type=F32)


def _dot_tn(a, b):
    return lax.dot_general(a, b, (((0,), (0,)), ((), ())), preferred_element_type=F32)


def _split(x, n):
    parts = []
    r = x
    for _ in range(n - 1):
        p = _bf(r)
        parts.append(p)
        r = r - p.astype(F32)
    parts.append(_bf(r))
    return parts


def _mm(a, b, passes, dot=_dot):
    if passes == 1:
        return dot(_bf(a), _bf(b))
    n = 2 if passes == 3 else 3
    ap, bp = _split(a, n), _split(b, n)
    acc = None
    for i in range(n):
        for j in range(n):
            if i + j < n:
                t = dot(ap[i], bp[j])
                acc = t if acc is None else acc + t
    return acc


def _mm_exact_lhs(a_bf, b, n, dot=_dot):
    acc = None
    for p in _split(b, n):
        t = dot(a_bf, p)
        acc = t if acc is None else acc + t
    return acc


def _mm_exact_rhs(a, b_bf, n):
    acc = None
    for p in _split(a, n):
        t = _dot(p, b_bf)
        acc = t if acc is None else acc + t
    return acc


def _rms(x, g, eps=RMS_EPS):
    return x * lax.rsqrt(jnp.mean(x * x, axis=-1, keepdims=True) + eps) * g


def _layer_norm(x, g, b, eps=LN_EPS):
    xc = x - jnp.mean(x, axis=-1, keepdims=True)
    return xc * lax.rsqrt(jnp.mean(xc * xc, axis=-1, keepdims=True) + eps) * g + b


def _sigmoid(x):
    return 1.0 / (1.0 + jnp.exp(-x))


def _silu(x):
    return x * _sigmoid(x)


def _gelu(x):
    return 0.5 * x * (1.0 + lax.erf(x * 0.7071067811865476))


def _softplus(x):
    return jnp.maximum(x, 0.0) + jnp.log1p(jnp.exp(-jnp.abs(x)))


def _iota2(shape, dim):
    return lax.broadcasted_iota(jnp.int32, shape, dim)


def _resident(shape):
    nd = len(shape)
    return pl.BlockSpec(shape, lambda *_: (0,) * nd, pipeline_mode=pl.Buffered(1))


def _params(*sem):
    return pltpu.CompilerParams(dimension_semantics=sem, vmem_limit_bytes=VMEM_LIMIT)


def _ab_kernel(x_ref, g_ref, w_ref, lng_ref, lnb_ref, wsp_ref, bsp_ref, a_ref, yb_ref, *, tm):
    h = _bf(_rms(x_ref[...], g_ref[...]))
    pa = _dot(h, w_ref[:, 0:2 * BRANCH_W])
    a_ref[...] = pa[:, :BRANCH_W] * _sigmoid(pa[:, BRANCH_W:])
    pb = _dot(h, w_ref[:, 2 * BRANCH_W:4 * BRANCH_W])
    u = _gelu(pb[:, :BRANCH_W])
    v = _bf(_layer_norm(_gelu(pb[:, BRANCH_W:]), lng_ref[...], lnb_ref[...]))
    row = _iota2((SGU_WIN, SGU_WIN), 0)
    col = _iota2((SGU_WIN, SGU_WIN), 1)
    allowed = (col // CHUNK) <= (row // CHUNK)
    gw = BRANCH_W // SGU_GROUPS
    for g in range(SGU_GROUPS):
        wg = _bf(jnp.where(allowed, wsp_ref[g], 0.0))
        bg = bsp_ref[g]
        for w in range(tm // SGU_WIN):
            rs = slice(w * SGU_WIN, (w + 1) * SGU_WIN)
            cs = slice(g * gw, (g + 1) * gw)
            mixed = _dot(wg, v[rs, cs]) + bg
            yb_ref[rs, cs] = (u[rs, cs] * mixed).astype(yb_ref.dtype)


def _proj_ab(x2, g, w_ab, sgu_ln_g, sgu_ln_b, sgu_w, sgu_b, tm):
    T, D = x2.shape
    kern = functools.partial(_ab_kernel, tm=tm)
    return pl.pallas_call(
        kern,
        grid=(T // tm,),
        in_specs=[
            pl.BlockSpec((tm, D), lambda i: (i, 0)),
            _resident((1, D)),
            _resident(w_ab.shape),
            _resident((1, BRANCH_W)),
            _resident((1, BRANCH_W)),
            _resident(sgu_w.shape),
            _resident((SGU_GROUPS, SGU_WIN, 1)),
        ],
        out_specs=[
            pl.BlockSpec((tm, BRANCH_W), lambda i: (i, 0)),
            pl.BlockSpec((tm, BRANCH_W), lambda i: (i, 0)),
        ],
        out_shape=[
            jax.ShapeDtypeStruct((T, BRANCH_W), F32),
            jax.ShapeDtypeStruct((T, BRANCH_W), BF16),
        ],
        compiler_params=_params("parallel"),
        name="proj_ab",
    )(x2, g.reshape(1, D), w_ab, sgu_ln_g.reshape(1, -1), sgu_ln_b.reshape(1, -1), sgu_w,
      sgu_b.reshape(SGU_GROUPS, SGU_WIN, 1))


def _conv_kernel(a_ref, halo_ref, w_ref, b_ref, lng_ref, lnb_ref, o_ref, buf_ref, sh_ref, *, tm, seq, rb):
    first = (pl.program_id(0) * tm) % seq == 0
    buf_ref[0:CONV_HALO, :] = jnp.where(first, 0.0, halo_ref[...])
    buf_ref[CONV_HALO:, :] = a_ref[...]
    n_sh = tm + CONV_HALO - SUBLANES
    for s in range(1, SUBLANES):
        sh_ref[s, 0:n_sh, :] = buf_ref[pl.ds(s, n_sh), :]
    off = CONV_HALO - (CONV_WIDTH - 1)
    for r in range(tm // rb):
        acc = jnp.zeros((rb, BRANCH_W), F32) + b_ref[...]
        for j in range(CONV_WIDTH):
            q, s = divmod(off + j, SUBLANES)
            rows = pl.ds(r * rb + q * SUBLANES, rb)
            tap = buf_ref[rows, :] if s == 0 else sh_ref[s, rows, :]
            acc = acc + w_ref[j:j + 1, :] * tap
        y = _layer_norm(acc, lng_ref[...], lnb_ref[...])
        o_ref[r * rb:(r + 1) * rb, :] = _silu(y).astype(o_ref.dtype)


def _conv(a, conv_w, conv_b, ln_g, ln_b, seq, tm):
    T, C = a.shape
    kern = functools.partial(_conv_kernel, tm=tm, seq=seq, rb=64)
    hb = tm // CONV_HALO
    return pl.pallas_call(
        kern,
        grid=(T // tm,),
        in_specs=[
            pl.BlockSpec((tm, C), lambda i: (i, 0)),
            pl.BlockSpec((CONV_HALO, C), lambda i: (jnp.maximum(i * hb - 1, 0), 0)),
            _resident(conv_w.shape),
            _resident((1, C)),
            _resident((1, C)),
            _resident((1, C)),
        ],
        out_specs=pl.BlockSpec((tm, C), lambda i: (i, 0)),
        out_shape=jax.ShapeDtypeStruct((T, C), BF16),
        scratch_shapes=[pltpu.VMEM((tm + CONV_HALO, C), F32), pltpu.VMEM((SUBLANES, tm + CONV_HALO, C), F32)],
        compiler_params=_params("parallel"),
        name="conv",
    )(a, a, conv_w, conv_b.reshape(1, C), ln_g.reshape(1, C), ln_b.reshape(1, C))


def _hg_kernel(x_ref, g_ref, w_ref, lb_ref, ng_ref, o_ref,
               q_s, lf_s, k_s, v_s, og_s, st_ref, *, tm, layer, depth):
    @pl.when(pl.program_id(1) == 0)
    def _():
        st_ref[...] = jnp.zeros_like(st_ref)

    W = HG_HEADS * HG_DK
    h = _bf(_rms(x_ref[0], g_ref[...]))
    p = _dot(h, w_ref[...])
    z = p[:, W:2 * W]
    lbp = lb_ref[...]
    e = jnp.exp(lbp - jnp.max(lbp, axis=0, keepdims=True))
    sm = e / jnp.sum(e, axis=0, keepdims=True)
    cs0 = sm[0:1]
    cs = cs0
    for l in range(1, layer + 1):
        cs = cs + sm[l:l + 1]
    lb = cs - cs0
    log_sig = -_softplus(-z)
    t1 = jnp.log(lb)
    t2 = jnp.log1p(-lb) + log_sig
    log_f = jnp.maximum(t1, t2) + jnp.log1p(jnp.exp(-jnp.abs(t1 - t2)))
    q_s[...] = _silu(p[:, 0:W])
    lf_s[...] = log_f
    k_s[...] = (1.0 - lb) * _sigmoid(-z)
    v_s[...] = p[:, 2 * W:3 * W]
    og_s[...] = _silu(p[:, 3 * W:4 * W]) * ng_ref[...]

    tri = _bf((_iota2((CHUNK, CHUNK), 1) <= _iota2((CHUNK, CHUNK), 0)).astype(F32))
    nsub = CHUNK // SUB
    rowc = _iota2((CHUNK, W), 0)
    row_s = _iota2((SUB, 1), 0)
    lane_c = _iota2((SUB, CHUNK), 1)

    def chunk(c, carry):
        for u in range(HG_UNROLL):
            one_chunk(pl.ds(pl.multiple_of((c * HG_UNROLL + u) * CHUNK, CHUNK), CHUNK))
        return carry

    def one_chunk(rows):
        H = range(HG_HEADS)
        hs = lambda a, hd: a[:, hd * HG_DK:(hd + 1) * HG_DK]
        q = q_s[rows, :]
        k = k_s[rows, :]
        v = _bf(v_s[rows, :])
        b = _mm_exact_lhs(tri, lf_s[rows, :], 3)
        bend = jnp.concatenate(
            [jnp.broadcast_to(b[(j + 1) * SUB - 1:(j + 1) * SUB, :], (SUB, W)) for j in range(nsub)], axis=0)
        khat = k * jnp.exp(bend - b)
        b_end = b[CHUNK - 1:CHUNK, :]
        q_in = _bf(q * jnp.exp(b))
        k_out = _bf(k * jnp.exp(b_end - b))
        st = [st_ref[hd] for hd in H]
        from_state = [_dot_nt(hs(q_in, hd), _bf(st[hd])) for hd in H]
        for hd in H:
            st_ref[hd] = st[hd] * hs(jnp.exp(b_end), hd) + _dot_tn(hs(v, hd), hs(k_out, hd))
        inter = []
        for i in range(1, nsub):
            rs = slice(i * SUB, (i + 1) * SUB)
            beta = b[i * SUB - 1:i * SUB, :]
            qt = _bf(q[rs] * jnp.exp(b[rs] - beta))
            rhs = _bf(jnp.where(rowc < i * SUB, khat * jnp.exp(beta - bend), 0.0))
            inter.append([_dot_nt(hs(qt, hd), hs(rhs, hd)) for hd in H])
        outs = []
        for hd in H:
            blocks = []
            for i in range(nsub):
                rs = slice(i * SUB, (i + 1) * SUB)
                bi = hs(b, hd)[rs]
                qi = hs(q, hd)[rs]
                ki = hs(k, hd)[rs]
                sc = jnp.zeros((SUB, CHUNK), F32) if i == 0 else inter[i - 1][hd]
                for s in range(SUB):
                    ee = jnp.exp(bi - bi[s:s + 1, :])
                    col = jnp.sum(qi * ki[s:s + 1, :] * ee, axis=1, keepdims=True)
                    col = jnp.where(row_s >= s, col, 0.0)
                    sc = jnp.where(lane_c == i * SUB + s, col, sc)
                blocks.append(sc)
            scores = jnp.concatenate(blocks, axis=0)
            out = _dot(_bf(scores), hs(v, hd)) + from_state[hd]
            ms = jnp.mean(out * out, axis=1, keepdims=True)
            outs.append(out * lax.rsqrt(ms + RMS_EPS))
        o_ref[0, rows, :] = (jnp.concatenate(outs, axis=1) * og_s[rows, :]).astype(o_ref.dtype)

    lax.fori_loop(0, tm // (CHUNK * HG_UNROLL), chunk, 0)


def _hgrn2(x, g, w_c, hg_lb, norm_g, layer, tm):
    B, S, D = x.shape
    W = HG_HEADS * HG_DK
    depth = hg_lb.shape[0]
    kern = functools.partial(_hg_kernel, tm=tm, layer=layer, depth=depth)
    return pl.pallas_call(
        kern,
        grid=(B, S // tm),
        in_specs=[
            pl.BlockSpec((1, tm, D), lambda b, i: (b, i, 0)),
            _resident((1, D)),
            _resident(w_c.shape),
            _resident(hg_lb.shape),
            _resident((1, W)),
        ],
        out_specs=pl.BlockSpec((1, tm, W), lambda b, i: (b, i, 0)),
        out_shape=jax.ShapeDtypeStruct((B, S, W), BF16),
        scratch_shapes=[pltpu.VMEM((tm, W), F32) for _ in range(5)]
        + [pltpu.VMEM((HG_HEADS, HG_DK, HG_DK), F32)],
        compiler_params=_params("parallel", "arbitrary"),
        name="hgrn2",
    )(x, g.reshape(1, D), w_c, hg_lb, norm_g.reshape(1, W))


RW_PASSES = 1
RW_UNROLL = 4


def _rw_chunks(chunks, h0s, tri, masks):
    strict, incl, same16, same32, eye = masks
    mm = functools.partial(_mm, passes=RW_PASSES)
    mm_nt = functools.partial(_mm, passes=RW_PASSES, dot=_dot_nt)
    mm_tn = functools.partial(_mm, passes=RW_PASSES, dot=_dot_tn)
    pre = []
    for (r, lw, k, v, kk, bv) in chunks:
        c = _mm_exact_lhs(tri, lw, 3)
        c_last = c[CHUNK - 1:CHUNK, :]
        e_neg = jnp.exp(-c)
        e_end = jnp.exp(c_last - c)
        pre.append(dict(rt=r * jnp.exp(c), p=kk * jnp.exp(c - lw), kin=k * e_neg, bin=bv * e_neg,
                        kg=k * e_end, bg=bv * e_end, gd=jnp.exp(c_last), v=v))
    P = [(ci, h) for ci in range(len(chunks)) for h in range(RW_HEADS)]
    sl = lambda name, q: pre[q[0]][name][:, q[1] * RW_N:(q[1] + 1) * RW_N]
    each = lambda f: {q: f(q) for q in P}
    lhs = each(lambda q: jnp.concatenate([sl("p", q), sl("rt", q)], axis=0))
    ab = each(lambda q: mm_nt(lhs[q], sl("bin", q)))
    ak = each(lambda q: mm_nt(lhs[q], sl("kin", q)))
    l_raw = each(lambda q: ab[q][:CHUNK])
    m = each(lambda q: jnp.where(strict, ak[q][:CHUNK], 0.0))
    arb = each(lambda q: jnp.where(incl, ab[q][CHUNK:], 0.0))
    ark = each(lambda q: jnp.where(incl, ak[q][CHUNK:], 0.0))
    n1 = each(lambda q: jnp.where(strict & same16, -l_raw[q], 0.0))
    n2 = each(lambda q: mm(n1[q], n1[q]))
    s2 = each(lambda q: eye + n1[q] + n2[q] + mm(n1[q], n2[q]))
    n4 = each(lambda q: mm(n2[q], n2[q]))
    s4 = each(lambda q: s2[q] + mm(s2[q], n4[q]))
    n8 = each(lambda q: mm(n4[q], n4[q]))
    t16 = each(lambda q: s4[q] + mm(s4[q], n8[q]))
    lo1 = each(lambda q: jnp.where(strict & same32 & jnp.logical_not(same16), l_raw[q], 0.0))
    x1 = each(lambda q: mm(t16[q], lo1[q]))
    t32 = each(lambda q: t16[q] - mm(x1[q], t16[q]))
    lo2 = each(lambda q: jnp.where(strict & jnp.logical_not(same32), l_raw[q], 0.0))
    x2 = each(lambda q: mm(t32[q], lo2[q]))
    t = each(lambda q: t32[q] - mm(x2[q], t32[q]))
    mv = each(lambda q: mm(m[q], sl("v", q)))
    pw = each(lambda q: mm(t[q], sl("p", q)))
    uv = each(lambda q: mm(t[q], mv[q]))
    rq = each(lambda q: sl("rt", q) - mm(arb[q], pw[q]))
    yv = each(lambda q: mm(ark[q], sl("v", q)) - mm(arb[q], uv[q]))
    g = each(lambda q: jnp.where(eye > 0.0, jnp.broadcast_to(sl("gd", q), (RW_N, RW_N)), 0.0)
             - mm_tn(sl("bg", q), pw[q]))
    hv = each(lambda q: mm_tn(sl("kg", q), sl("v", q)) - mm_tn(sl("bg", q), uv[q]))
    hs = list(h0s)
    ys = []
    for ci in range(len(chunks)):
        ys.append(jnp.concatenate([mm(rq[(ci, h)], hs[h]) + yv[(ci, h)] for h in range(RW_HEADS)], axis=1))
        hs = [mm(g[(ci, h)], hs[h]) + hv[(ci, h)] for h in range(RW_HEADS)]
    return ys, hs


def _rw_kernel(*refs, tm, has_vres):
    if has_vres:
        (x_ref, g_ref, w_ref, mur_ref, mul_ref, w0_ref, w1_ref, w2_ref, a0_ref, a1_ref, a2_ref,
         g1_ref, g2_ref, kk_ref, ka_ref, rk_ref, lng_ref, lnb_ref, ones_ref,
         vf_ref, muv_ref, v0_ref, v1_ref, v2_ref,
         o_ref, r_s, lw_s, k_s, v_s, kk_s, bv_s, y_s, ch_s, cp_s, st_ref) = refs
    else:
        (x_ref, g_ref, w_ref, mur_ref, mul_ref, w0_ref, w1_ref, w2_ref, a0_ref, a1_ref, a2_ref,
         g1_ref, g2_ref, kk_ref, ka_ref, rk_ref, lng_ref, lnb_ref, ones_ref,
         o_ref, vf_out_ref, r_s, lw_s, k_s, v_s, kk_s, bv_s, y_s, ch_s, cp_s, st_ref) = refs
    W = RW_HEADS * RW_N

    @pl.when(pl.program_id(1) == 0)
    def _():
        st_ref[...] = jnp.zeros_like(st_ref)
        ch_s[...] = jnp.zeros_like(ch_s)
        cp_s[...] = jnp.zeros_like(cp_s)

    h = _rms(x_ref[0], g_ref[...])
    p = _dot(_bf(h), w_ref[...])
    row_d = _iota2(h.shape, 0)
    row_p = _iota2(p.shape, 0)
    hs = jnp.where(row_d == 0, ch_s[...], pltpu.roll(h, 1, 0))
    ps = jnp.where(row_p == 0, cp_s[...], pltpu.roll(p, 1, 0))
    ch_s[...] = h[tm - 1:tm, :]
    cp_s[...] = p[tm - 1:tm, :]
    dh = hs - h
    pm = p + (ps - p) * mur_ref[...]
    r = pm[:, 0:W]
    k = pm[:, W:2 * W]
    v = pm[:, 2 * W:3 * W]
    xw = _bf(h + dh * mul_ref[0:1, :])
    xa = _bf(h + dh * mul_ref[1:2, :])
    xg = _bf(h + dh * mul_ref[2:3, :])
    wl = w0_ref[...] + _dot(_bf(jnp.tanh(_dot(xw, w1_ref[...]))), w2_ref[...])
    w_log = -_softplus(-wl) - 0.5
    lw = -jnp.exp(w_log)
    if has_vres:
        xv = _bf(h + dh * muv_ref[...])
        mixv = _sigmoid(v0_ref[...] + _dot(_bf(_dot(xv, v1_ref[...])), v2_ref[...]))
        v = v + (vf_ref[0] - v) * mixv
    else:
        vf_out_ref[0] = v
    a = _sigmoid(a0_ref[...] + _dot(_bf(_dot(xa, a1_ref[...])), a2_ref[...]))
    gate = _dot(_bf(_sigmoid(_dot(xg, g1_ref[...]))), g2_ref[...])
    ones = ones_ref[...]
    kk = k * kk_ref[...]
    ssq = _mm_exact_rhs(kk * kk, ones, 1)
    kk = kk * lax.rsqrt(jnp.maximum(ssq, 1e-24))
    k = k * (1.0 + (a - 1.0) * ka_ref[...])
    r_s[...] = r
    lw_s[...] = lw
    k_s[...] = k
    v_s[...] = v
    kk_s[...] = kk
    bv_s[...] = kk * a

    ri = _iota2((CHUNK, CHUNK), 0)
    ci = _iota2((CHUNK, CHUNK), 1)
    strict = ci < ri
    incl = ci <= ri
    same16 = (ri // 16) == (ci // 16)
    same32 = (ri // 32) == (ci // 32)
    eye = (ri == ci).astype(F32)
    masks = (strict, incl, same16, same32, eye)
    tri = _bf(incl.astype(F32))

    def chunk(c, carry):
        rows = [pl.ds(pl.multiple_of((c * RW_UNROLL + u) * CHUNK, CHUNK), CHUNK) for u in range(RW_UNROLL)]
        ys, h1 = _rw_chunks([(r_s[rw, :], lw_s[rw, :], k_s[rw, :], v_s[rw, :], kk_s[rw, :], bv_s[rw, :]) for rw in rows],
                            [st_ref[hd] for hd in range(RW_HEADS)], tri, masks)
        for hd in range(RW_HEADS):
            st_ref[hd] = h1[hd]
        for rw, y in zip(rows, ys):
            y_s[rw, :] = y
        return carry

    lax.fori_loop(0, tm // (CHUNK * RW_UNROLL), chunk, 0)

    y = y_s[...]
    inv_n = 1.0 / RW_N
    mean = _mm_exact_rhs(y, ones, 2) * inv_n
    yc = y - mean
    var = _mm_exact_rhs(yc * yc, ones, 1) * inv_n
    yn = yc * lax.rsqrt(var + RW_GN_EPS) * lng_ref[...] + lnb_ref[...]
    bonus = _mm_exact_rhs(r_s[...] * k_s[...] * rk_ref[...], ones, 1)
    yn = yn + bonus * v_s[...]
    o_ref[0] = (yn * gate).astype(o_ref.dtype)


def _rwkv7(x, g, w_d, mu_rkv, mu_lora, w0, w1, w2, a0, a1, a2, g1, g2, k_k, k_a, r_k, ln_g, ln_b,
           v_first, vres, tm):
    B, S, D = x.shape
    W = RW_HEADS * RW_N
    has_vres = vres is not None
    hid = jnp.arange(W) // RW_N
    ones = (hid[:, None] == hid[None, :]).astype(BF16)
    row = lambda t: t.reshape(1, -1)
    args = [x, row(g), w_d, row(mu_rkv), mu_lora, row(w0), _bf(w1), _bf(w2), row(a0), _bf(a1), _bf(a2),
            _bf(g1), _bf(g2), row(k_k), row(k_a), row(r_k), row(ln_g), row(ln_b), ones]
    in_specs = [pl.BlockSpec((1, tm, D), lambda b, i: (b, i, 0))] + [_resident(t.shape) for t in args[1:]]
    tile_w = pl.BlockSpec((1, tm, W), lambda b, i: (b, i, 0))
    if has_vres:
        mu_v, v0, v1, v2 = vres
        extra = [row(mu_v), row(v0), _bf(v1), _bf(v2)]
        args += [v_first] + extra
        in_specs += [tile_w] + [_resident(t.shape) for t in extra]
        out_specs = tile_w
        out_shape = jax.ShapeDtypeStruct((B, S, W), BF16)
    else:
        out_specs = [tile_w, tile_w]
        out_shape = [jax.ShapeDtypeStruct((B, S, W), BF16), jax.ShapeDtypeStruct((B, S, W), F32)]
    kern = functools.partial(_rw_kernel, tm=tm, has_vres=has_vres)
    res = pl.pallas_call(
        kern,
        grid=(B, S // tm),
        in_specs=in_specs,
        out_specs=out_specs,
        out_shape=out_shape,
        scratch_shapes=[pltpu.VMEM((tm, W), F32) for _ in range(7)]
        + [pltpu.VMEM((1, D), F32), pltpu.VMEM((1, 3 * W), F32), pltpu.VMEM((RW_HEADS, RW_N, RW_N), F32)],
        compiler_params=_params("parallel", "arbitrary"),
        name="rwkv7",
    )(*args)
    if has_vres:
        return res, v_first
    return res[0], res[1]


def _merge_kernel(x_ref, g_ref, ya_ref, yb_ref, yc_ref, yd_ref, wg_ref, bg_ref, wb_ref, wo_ref, o_ref):
    x = x_ref[...]
    D = x.shape[1]
    h = _bf(_rms(x, g_ref[...]))
    mixed = None
    for j, y_ref in enumerate((ya_ref, yb_ref, yc_ref, yd_ref)):
        gate = _sigmoid(_dot(h, wg_ref[:, j * D:(j + 1) * D]) + bg_ref[:, j * D:(j + 1) * D])
        t = gate * _dot(y_ref[...], wb_ref[j])
        mixed = t if mixed is None else mixed + t
    o_ref[...] = x + _dot(_bf(mixed), wo_ref[...])


def _merge(x2, g, ys, w_gate, b_gate, w_branch, w_out, tm):
    T, D = x2.shape
    tile = lambda w: pl.BlockSpec((tm, w), lambda i: (i, 0))
    return pl.pallas_call(
        _merge_kernel,
        grid=(T // tm,),
        in_specs=[tile(D), _resident((1, D))] + [tile(BRANCH_W)] * 4
        + [_resident(w_gate.shape), _resident(b_gate.shape), _resident(w_branch.shape), _resident(w_out.shape)],
        out_specs=tile(D),
        out_shape=jax.ShapeDtypeStruct((T, D), F32),
        compiler_params=_params("parallel"),
        name="merge",
    )(x2, g.reshape(1, D), *ys, w_gate, b_gate, w_branch, w_out)


def _ffn_kernel(x_ref, g_ref, wg_ref, wu_ref, wd_ref, fg_ref, o_ref, *, hc, final):
    x = x_ref[...]
    h = _bf(_rms(x, g_ref[...]))
    hidden = wg_ref.shape[1]
    acc = x
    for c in range(hidden // hc):
        cs = slice(c * hc, (c + 1) * hc)
        act = _silu(_dot(h, wg_ref[:, cs])) * _dot(h, wu_ref[:, cs])
        acc = acc + _dot(_bf(act), wd_ref[cs, :])
    if final:
        acc = _rms(acc, fg_ref[...])
    o_ref[...] = acc


def _ffn(x2, g, w_gate, w_up, w_down, final_g, final, tm):
    T, D = x2.shape
    hidden = w_gate.shape[1]
    kern = functools.partial(_ffn_kernel, hc=256, final=final)
    return pl.pallas_call(
        kern,
        grid=(T // tm,),
        in_specs=[pl.BlockSpec((tm, D), lambda i: (i, 0)), _resident((1, D)),
                  _resident(w_gate.shape), _resident(w_up.shape), _resident(w_down.shape), _resident((1, D))],
        out_specs=pl.BlockSpec((tm, D), lambda i: (i, 0)),
        out_shape=jax.ShapeDtypeStruct((T, D), F32),
        compiler_params=_params("parallel"),
        name="ffn",
    )(x2, g.reshape(1, D), w_gate, w_up, w_down, final_g.reshape(1, D))


def kernel(x, norm_mix_g, w_in, w_gate, b_gate, conv_w, conv_b, conv_ln_g, conv_ln_b, sgu_ln_g, sgu_ln_b, sgu_w, sgu_b, hg_lb, hg_norm_g, rw_mu_rkv, rw_mu_lora, rw_w0, rw_w1, rw_w2, rw_a0, rw_a1, rw_a2, rw_g1, rw_g2, rw_k_k, rw_k_a, rw_r_k, rw_ln_g, rw_ln_b, rw_mu_vres, rw_v0, rw_v1, rw_v2, w_branch, w_out, norm_ffn_g, w_ffn_gate, w_ffn_up, w_ffn_down, final_norm_g):
    B, S, D = x.shape
    depth = w_in.shape[0]
    T = B * S
    tm = 256
    tm_mm = 512
    n_ab = 4 * BRANCH_W
    n_c = n_ab + 4 * HG_HEADS * HG_DK
    v_first = None
    for l in range(depth):
        w_in_l = _bf(w_in[l])
        x2 = x.reshape(T, D)
        a, y_b = _proj_ab(x2, norm_mix_g[l], w_in_l[:, :n_ab], sgu_ln_g[l], sgu_ln_b[l], sgu_w[l], sgu_b[l], tm_mm)
        y_a = _conv(a, conv_w[l], conv_b[l], conv_ln_g[l], conv_ln_b[l], S, tm)
        y_c = _hgrn2(x, norm_mix_g[l], w_in_l[:, n_ab:n_c], hg_lb, hg_norm_g[l], l, tm)
        vres = None if l == 0 else (rw_mu_vres[l - 1], rw_v0[l - 1], rw_v1[l - 1], rw_v2[l - 1])
        y_d, v_first = _rwkv7(x, norm_mix_g[l], w_in_l[:, n_c:], rw_mu_rkv[l], rw_mu_lora[l],
                              rw_w0[l], rw_w1[l], rw_w2[l], rw_a0[l], rw_a1[l], rw_a2[l],
                              rw_g1[l], rw_g2[l], rw_k_k[l], rw_k_a[l], rw_r_k[l], rw_ln_g[l], rw_ln_b[l],
                              v_first, vres, tm)
        wg = _bf(jnp.concatenate([w_gate[l, j] for j in range(4)], axis=1))
        bg = b_gate[l].reshape(1, -1)
        x2 = _merge(x2, norm_mix_g[l], (y_a, y_b, y_c.reshape(T, -1), y_d.reshape(T, -1)),
                    wg, bg, _bf(w_branch[l]), _bf(w_out[l]), tm_mm)
        x2 = _ffn(x2, norm_ffn_g[l], _bf(w_ffn_gate[l]), _bf(w_ffn_up[l]), _bf(w_ffn_down[l]),
                  final_norm_g, l == depth - 1, tm_mm)
        x = x2.reshape(B, S, D)
    return x
```

```python
import functools

import jax
import jax.numpy as jnp
from jax import lax
from jax.experimental import pallas as pl
from jax.experimental.pallas import tpu as pltpu

F32 = jnp.float32
BF16 = jnp.bfloat16

CHUNK = 64
SUB = 8
BRANCH_W = 512
CONV_WIDTH = 31
SUBLANES = 8
CONV_HALO = 32
SGU_GROUPS = 4
SGU_WIN = 128
HG_HEADS = 4
HG_DK = 128
RW_HEADS = 8
RW_N = 64
RMS_EPS = 1e-6
LN_EPS = 1e-5
RW_GN_EPS = 64e-5
VMEM_LIMIT = 56 * 1024 * 1024


def _bf(x):
    return x.astype(BF16)


def _dot(a, b):
    return jnp.dot(a, b, preferred_element_type=F32)


def _dot_nt(a, b):
    return lax.dot_general(a, b, (((1,), (1,)), ((), ())), preferred_element_type=F32)


def _dot_tn(a, b):
    return lax.dot_general(a, b, (((0,), (0,)), ((), ())), preferred_element_type=F32)


def _split(x, n):
    parts = []
    r = x
    for _ in range(n - 1):
        p = _bf(r)
        parts.append(p)
        r = r - p.astype(F32)
    parts.append(_bf(r))
    return parts


def _mm(a, b, passes, dot=_dot):
    if passes == 1:
        return dot(_bf(a), _bf(b))
    n = 2 if passes == 3 else 3
    ap, bp = _split(a, n), _split(b, n)
    acc = None
    for i in range(n):
        for j in range(n):
            if i + j < n:
                t = dot(ap[i], bp[j])
                acc = t if acc is None else acc + t
    return acc


def _mm_exact_lhs(a_bf, b, n, dot=_dot):
    acc = None
    for p in _split(b, n):
        t = dot(a_bf, p)
        acc = t if acc is None else acc + t
    return acc


def _rms(x, g, eps=RMS_EPS):
    return x * lax.rsqrt(jnp.mean(x * x, axis=-1, keepdims=True) + eps) * g


def _layer_norm(x, g, b, eps=LN_EPS):
    xc = x - jnp.mean(x, axis=-1, keepdims=True)
    return xc * lax.rsqrt(jnp.mean(xc * xc, axis=-1, keepdims=True) + eps) * g + b


def _sigmoid(x):
    return 1.0 / (1.0 + jnp.exp(-x))


def _silu(x):
    return x * _sigmoid(x)


def _gelu(x):
    return 0.5 * x * (1.0 + lax.erf(x * 0.7071067811865476))


def _softplus(x):
    return jnp.maximum(x, 0.0) + jnp.log1p(jnp.exp(-jnp.abs(x)))


def _iota2(shape, dim):
    return lax.broadcasted_iota(jnp.int32, shape, dim)


def _resident(shape):
    nd = len(shape)
    return pl.BlockSpec(shape, lambda *_: (0,) * nd, pipeline_mode=pl.Buffered(1))


def _params(*sem):
    return pltpu.CompilerParams(dimension_semantics=sem, vmem_limit_bytes=VMEM_LIMIT)


def _ab_kernel(x_ref, g_ref, w_ref, lng_ref, lnb_ref, wsp_ref, bsp_ref, a_ref, yb_ref, *, tm):
    h = _bf(_rms(x_ref[...], g_ref[...]))
    pa = _dot(h, w_ref[:, 0:2 * BRANCH_W])
    a_ref[...] = pa[:, :BRANCH_W] * _sigmoid(pa[:, BRANCH_W:])
    pb = _dot(h, w_ref[:, 2 * BRANCH_W:4 * BRANCH_W])
    u = _gelu(pb[:, :BRANCH_W])
    v = _bf(_layer_norm(_gelu(pb[:, BRANCH_W:]), lng_ref[...], lnb_ref[...]))
    row = _iota2((SGU_WIN, SGU_WIN), 0)
    col = _iota2((SGU_WIN, SGU_WIN), 1)
    allowed = (col // CHUNK) <= (row // CHUNK)
    gw = BRANCH_W // SGU_GROUPS
    for g in range(SGU_GROUPS):
        wg = _bf(jnp.where(allowed, wsp_ref[g], 0.0))
        bg = bsp_ref[g]
        for w in range(tm // SGU_WIN):
            rs = slice(w * SGU_WIN, (w + 1) * SGU_WIN)
            cs = slice(g * gw, (g + 1) * gw)
            mixed = _dot(wg, v[rs, cs]) + bg
            yb_ref[rs, cs] = (u[rs, cs] * mixed).astype(yb_ref.dtype)


def _proj_ab(x2, g, w_ab, sgu_ln_g, sgu_ln_b, sgu_w, sgu_b, tm):
    T, D = x2.shape
    kern = functools.partial(_ab_kernel, tm=tm)
    return pl.pallas_call(
        kern,
        grid=(T // tm,),
        in_specs=[
            pl.BlockSpec((tm, D), lambda i: (i, 0)),
            _resident((1, D)),
            _resident(w_ab.shape),
            _resident((1, BRANCH_W)),
            _resident((1, BRANCH_W)),
            _resident(sgu_w.shape),
            _resident((SGU_GROUPS, SGU_WIN, 1)),
        ],
        out_specs=[
            pl.BlockSpec((tm, BRANCH_W), lambda i: (i, 0)),
            pl.BlockSpec((tm, BRANCH_W), lambda i: (i, 0)),
        ],
        out_shape=[
            jax.ShapeDtypeStruct((T, BRANCH_W), F32),
            jax.ShapeDtypeStruct((T, BRANCH_W), BF16),
        ],
        compiler_params=_params("parallel"),
        name="proj_ab",
    )(x2, g.reshape(1, D), w_ab, sgu_ln_g.reshape(1, -1), sgu_ln_b.reshape(1, -1), sgu_w,
      sgu_b.reshape(SGU_GROUPS, SGU_WIN, 1))


CONV_RB = 128


def _conv_pieces(a_ref, halo_ref, w_ref, b_ref, lng_ref, lnb_ref, buf_ref, sh_ref, ya_ref, *, tm, seq):
    first = (pl.program_id(0) * tm) % seq == 0
    buf_ref[0:CONV_HALO, :] = jnp.where(first, 0.0, halo_ref[...])
    buf_ref[CONV_HALO:, :] = a_ref[...]
    n_sh = tm + CONV_HALO - SUBLANES
    for s in range(1, SUBLANES):
        sh_ref[s, 0:n_sh, :] = buf_ref[pl.ds(s, n_sh), :]
        yield
    off = CONV_HALO - (CONV_WIDTH - 1)
    for r in range(tm // CONV_RB):
        acc = jnp.zeros((CONV_RB, BRANCH_W), F32) + b_ref[...]
        for j in range(CONV_WIDTH):
            q, s = divmod(off + j, SUBLANES)
            rows = pl.ds(r * CONV_RB + q * SUBLANES, CONV_RB)
            tap = buf_ref[rows, :] if s == 0 else sh_ref[s, rows, :]
            acc = acc + w_ref[j:j + 1, :] * tap
        y = _layer_norm(acc, lng_ref[...], lnb_ref[...])
        ya_ref[r * CONV_RB:(r + 1) * CONV_RB, :] = _silu(y).astype(ya_ref.dtype)
        yield


RW_PASSES = 1
HG_STEPS_PER_TICK = 4


def _hg_pieces(h, w_ref, lb_ref, ng_ref, o_ref, q_s, lf_s, k_s, v_s, og_s, st_ref, *, tm, layer):
    W = HG_HEADS * HG_DK
    p = _dot(h, w_ref[...])
    z = p[:, W:2 * W]
    lbp = lb_ref[...]
    e = jnp.exp(lbp - jnp.max(lbp, axis=0, keepdims=True))
    sm = e / jnp.sum(e, axis=0, keepdims=True)
    cs0 = sm[0:1]
    cs = cs0
    for l in range(1, layer + 1):
        cs = cs + sm[l:l + 1]
    lb = cs - cs0
    log_sig = -_softplus(-z)
    t1 = jnp.log(lb)
    t2 = jnp.log1p(-lb) + log_sig
    log_f = jnp.maximum(t1, t2) + jnp.log1p(jnp.exp(-jnp.abs(t1 - t2)))
    q_s[...] = _silu(p[:, 0:W])
    lf_s[...] = log_f
    k_s[...] = (1.0 - lb) * _sigmoid(-z)
    v_s[...] = p[:, 2 * W:3 * W]
    og_s[...] = _silu(p[:, 3 * W:4 * W]) * ng_ref[...]
    yield

    tri = _bf((_iota2((CHUNK, CHUNK), 1) <= _iota2((CHUNK, CHUNK), 0)).astype(F32))
    nsub = CHUNK // SUB
    rowc = _iota2((CHUNK, W), 0)
    row_s = _iota2((SUB, CHUNK), 0)
    lane_c = _iota2((SUB, CHUNK), 1)
    place = [(lane_c == c) & (row_s >= c % SUB) for c in range(CHUNK)]
    H = range(HG_HEADS)
    hs = lambda a, hd: a[:, hd * HG_DK:(hd + 1) * HG_DK]
    for c in range(tm // CHUNK):
        rows = pl.ds(c * CHUNK, CHUNK)
        q = q_s[rows, :]
        k = k_s[rows, :]
        v = _bf(v_s[rows, :])
        b = _mm_exact_lhs(tri, lf_s[rows, :], 3)
        bend = jnp.concatenate(
            [jnp.broadcast_to(b[(j + 1) * SUB - 1:(j + 1) * SUB, :], (SUB, W)) for j in range(nsub)], axis=0)
        khat = k * jnp.exp(bend - b)
        b_end = b[CHUNK - 1:CHUNK, :]
        q_in = _bf(q * jnp.exp(b))
        k_out = _bf(k * jnp.exp(b_end - b))
        yield
        st = [st_ref[hd] for hd in H]
        from_state = [_dot_nt(hs(q_in, hd), _bf(st[hd])) for hd in H]
        for hd in H:
            st_ref[hd] = st[hd] * hs(jnp.exp(b_end), hd) + _dot_tn(hs(v, hd), hs(k_out, hd))
        yield
        inter = []
        for i in range(1, nsub):
            rs = slice(i * SUB, (i + 1) * SUB)
            beta = b[i * SUB - 1:i * SUB, :]
            qt = _bf(q[rs] * jnp.exp(b[rs] - beta))
            rhs = _bf(jnp.where(rowc < i * SUB, khat * jnp.exp(beta - bend), 0.0))
            inter.append([_dot_nt(hs(qt, hd), hs(rhs, hd)) for hd in H])
            yield
        outs = []
        for hd in H:
            blocks = []
            for i in range(nsub):
                rs = slice(i * SUB, (i + 1) * SUB)
                bi = hs(b, hd)[rs]
                qi = hs(q, hd)[rs]
                ki = hs(k, hd)[rs]
                sc = jnp.zeros((SUB, CHUNK), F32) if i == 0 else inter[i - 1][hd]
                for s in range(SUB):
                    ee = jnp.exp(bi - bi[s:s + 1, :])
                    col = jnp.sum(qi * ki[s:s + 1, :] * ee, axis=1, keepdims=True)
                    sc = jnp.where(place[i * SUB + s], col, sc)
                blocks.append(sc)
                yield
            scores = jnp.concatenate(blocks, axis=0)
            out = _dot(_bf(scores), hs(v, hd)) + from_state[hd]
            ms = jnp.mean(out * out, axis=1, keepdims=True)
            outs.append(out * lax.rsqrt(ms + RMS_EPS))
        o_ref[0, rows, :] = (jnp.concatenate(outs, axis=1) * og_s[rows, :]).astype(o_ref.dtype)
        yield


def _rw_chunks(chunks, h0s, tri, masks, tick):
    strict, incl, same16, same32, eye = masks
    mm = functools.partial(_mm, passes=RW_PASSES)
    mm_nt = functools.partial(_mm, passes=RW_PASSES, dot=_dot_nt)
    mm_tn = functools.partial(_mm, passes=RW_PASSES, dot=_dot_tn)
    pre = []
    for (r, lw, k, v, kk, bv) in chunks:
        c = _mm_exact_lhs(tri, lw, 3)
        c_last = c[CHUNK - 1:CHUNK, :]
        e_neg = jnp.exp(-c)
        e_end = jnp.exp(c_last - c)
        pre.append(dict(rt=r * jnp.exp(c), p=kk * jnp.exp(c - lw), kin=k * e_neg, bin=bv * e_neg,
                        kg=k * e_end, bg=bv * e_end, gd=jnp.exp(c_last), v=v))
    P = [(ci, h) for ci in range(len(chunks)) for h in range(RW_HEADS)]
    sl = lambda name, q: pre[q[0]][name][:, q[1] * RW_N:(q[1] + 1) * RW_N]

    def each(f):
        out = {q: f(q) for q in P}
        tick()
        return out

    lhs = each(lambda q: jnp.concatenate([sl("p", q), sl("rt", q)], axis=0))
    ab = each(lambda q: mm_nt(lhs[q], sl("bin", q)))
    ak = each(lambda q: mm_nt(lhs[q], sl("kin", q)))
    l_raw = each(lambda q: ab[q][:CHUNK])
    m = each(lambda q: jnp.where(strict, ak[q][:CHUNK], 0.0))
    arb = each(lambda q: jnp.where(incl, ab[q][CHUNK:], 0.0))
    ark = each(lambda q: jnp.where(incl, ak[q][CHUNK:], 0.0))
    n1 = each(lambda q: jnp.where(strict & same16, -l_raw[q], 0.0))
    rows2 = lambda a, b: jnp.concatenate([a, b], axis=0)
    n2 = each(lambda q: mm(n1[q], n1[q]))
    a2 = each(lambda q: mm(rows2(n1[q], n2[q]), n2[q]))
    s2 = each(lambda q: eye + n1[q] + n2[q] + a2[q][:CHUNK])
    n4 = each(lambda q: a2[q][CHUNK:])
    a4 = each(lambda q: mm(rows2(s2[q], n4[q]), n4[q]))
    s4 = each(lambda q: s2[q] + a4[q][:CHUNK])
    n8 = each(lambda q: a4[q][CHUNK:])
    t16 = each(lambda q: s4[q] + mm(s4[q], n8[q]))
    lo1 = each(lambda q: jnp.where(strict & same32 & jnp.logical_not(same16), l_raw[q], 0.0))
    x1 = each(lambda q: mm(t16[q], lo1[q]))
    t32 = each(lambda q: t16[q] - mm(x1[q], t16[q]))
    lo2 = each(lambda q: jnp.where(strict & jnp.logical_not(same32), l_raw[q], 0.0))
    x2 = each(lambda q: mm(t32[q], lo2[q]))
    t = each(lambda q: t32[q] - mm(x2[q], t32[q]))
    mv = each(lambda q: mm(m[q], sl("v", q)))
    z = each(lambda q: mm(t[q], jnp.concatenate([sl("p", q), mv[q]], axis=1)))
    az = each(lambda q: mm(arb[q], z[q]))
    bz = each(lambda q: mm_tn(sl("bg", q), z[q]))
    rq = each(lambda q: sl("rt", q) - az[q][:, :RW_N])
    yv = each(lambda q: mm(ark[q], sl("v", q)) - az[q][:, RW_N:])
    g = each(lambda q: jnp.where(eye > 0.0, jnp.broadcast_to(sl("gd", q), (RW_N, RW_N)), 0.0) - bz[q][:, :RW_N])
    hv = each(lambda q: mm_tn(sl("kg", q), sl("v", q)) - bz[q][:, RW_N:])
    hs = list(h0s)
    ys = []
    for ci in range(len(chunks)):
        yh = [mm(rows2(rq[(ci, h)], g[(ci, h)]), hs[h]) for h in range(RW_HEADS)]
        ys.append(jnp.concatenate([yh[h][:CHUNK] + yv[(ci, h)] for h in range(RW_HEADS)], axis=1))
        hs = [yh[h][CHUNK:] + hv[(ci, h)] for h in range(RW_HEADS)]
    return ys, hs


def _head_sums(x, ones_half):
    hw = ones_half.shape[0]
    return jnp.concatenate([_dot(_bf(x[:, j * hw:(j + 1) * hw]), ones_half) for j in range(x.shape[1] // hw)], axis=1)


def _cd_kernel(*refs, tm, layer, has_vres):
    (x_ref, g_ref, wc_ref, lb_ref, ng_ref, w_ref, mur_ref, mul_ref, w0_ref, w1_ref, w2_ref, a0_ref, a1_ref, a2_ref,
     g1_ref, g2_ref, kk_ref, ka_ref, rk_ref, lng_ref, lnb_ref, ones_ref) = refs[:22]
    if has_vres:
        vf_ref, muv_ref, v0_ref, v1_ref, v2_ref = refs[22:27]
        oc_ref, o_ref = refs[27:29]
        scratch = refs[29:]
    else:
        oc_ref, o_ref, vf_out_ref = refs[22:25]
        scratch = refs[25:]
    (hq_s, hlf_s, hk_s, hv_s, hog_s, hst_ref, r_s, lw_s, k_s, v_s, kk_s, bv_s, ch_s, cp_s, st_ref) = scratch
    W = RW_HEADS * RW_N

    @pl.when(pl.program_id(1) == 0)
    def _():
        hst_ref[...] = jnp.zeros_like(hst_ref)
        st_ref[...] = jnp.zeros_like(st_ref)
        ch_s[...] = jnp.zeros_like(ch_s)
        cp_s[...] = jnp.zeros_like(cp_s)

    h = _rms(x_ref[0], g_ref[...])
    hb = _bf(h)
    p = _dot(hb, w_ref[...])
    row_d = _iota2(h.shape, 0)
    row_p = _iota2(p.shape, 0)
    hs = jnp.where(row_d == 0, ch_s[...], pltpu.roll(h, 1, 0))
    ps = jnp.where(row_p == 0, cp_s[...], pltpu.roll(p, 1, 0))
    ch_s[...] = h[tm - 1:tm, :]
    cp_s[...] = p[tm - 1:tm, :]
    dh = hs - h
    pm = p + (ps - p) * mur_ref[...]
    r = pm[:, 0:W]
    k = pm[:, W:2 * W]
    v = pm[:, 2 * W:3 * W]
    xw = _bf(h + dh * mul_ref[0:1, :])
    xa = _bf(h + dh * mul_ref[1:2, :])
    xg = _bf(h + dh * mul_ref[2:3, :])
    wl = w0_ref[...] + _dot(_bf(jnp.tanh(_dot(xw, w1_ref[...]))), w2_ref[...])
    w_log = -_softplus(-wl) - 0.5
    lw = -jnp.exp(w_log)
    if has_vres:
        xv = _bf(h + dh * muv_ref[...])
        mixv = _sigmoid(v0_ref[...] + _dot(_bf(_dot(xv, v1_ref[...])), v2_ref[...]))
        v = v + (vf_ref[0] - v) * mixv
    else:
        vf_out_ref[0] = v
    a = _sigmoid(a0_ref[...] + _dot(_bf(_dot(xa, a1_ref[...])), a2_ref[...]))
    gate = _dot(_bf(_sigmoid(_dot(xg, g1_ref[...]))), g2_ref[...])
    ones = ones_ref[...]
    kk = k * kk_ref[...]
    ssq = _head_sums(kk * kk, ones)
    kk = kk * lax.rsqrt(jnp.maximum(ssq, 1e-24))
    k = k * (1.0 + (a - 1.0) * ka_ref[...])
    r_s[...] = r
    lw_s[...] = lw
    k_s[...] = k
    v_s[...] = v
    kk_s[...] = kk
    bv_s[...] = kk * a

    hg = _hg_pieces(hb, wc_ref, lb_ref, ng_ref, oc_ref, hq_s, hlf_s, hk_s, hv_s, hog_s, hst_ref, tm=tm, layer=layer)
    next(hg)

    def tick():
        for _ in range(HG_STEPS_PER_TICK):
            next(hg, None)

    ri = _iota2((CHUNK, CHUNK), 0)
    ci = _iota2((CHUNK, CHUNK), 1)
    strict = ci < ri
    incl = ci <= ri
    same16 = (ri // 16) == (ci // 16)
    same32 = (ri // 32) == (ci // 32)
    eye = (ri == ci).astype(F32)
    masks = (strict, incl, same16, same32, eye)
    tri = _bf(incl.astype(F32))
    rows = [pl.ds(u * CHUNK, CHUNK) for u in range(tm // CHUNK)]
    ys, h1 = _rw_chunks([(r_s[rw, :], lw_s[rw, :], k_s[rw, :], v_s[rw, :], kk_s[rw, :], bv_s[rw, :]) for rw in rows],
                        [st_ref[hd] for hd in range(RW_HEADS)], tri, masks, tick)
    for hd in range(RW_HEADS):
        st_ref[hd] = h1[hd]
    for _ in hg:
        pass

    y = jnp.concatenate(ys, axis=0)
    inv_n = 1.0 / RW_N
    y_hi = _bf(y).astype(F32)
    mean = (_head_sums(y_hi, ones) + _head_sums(y - y_hi, ones)) * inv_n
    yc = y - mean
    var = _head_sums(yc * yc, ones) * inv_n
    yn = yc * lax.rsqrt(var + RW_GN_EPS) * lng_ref[...] + lnb_ref[...]
    bonus = _head_sums(r * k * rk_ref[...], ones)
    yn = yn + bonus * v
    o_ref[0] = (yn * gate).astype(o_ref.dtype)


def _mixers_cd(x, g, w_c, hg_lb, hg_norm_g, layer, w_d, mu_rkv, mu_lora, w0, w1, w2, a0, a1, a2, g1, g2,
               k_k, k_a, r_k, ln_g, ln_b, v_first, vres, tm):
    B, S, D = x.shape
    W = RW_HEADS * RW_N
    WC = HG_HEADS * HG_DK
    has_vres = vres is not None
    hid = jnp.arange(W // 2) // RW_N
    ones = (hid[:, None] == hid[None, :]).astype(BF16)
    row = lambda t: t.reshape(1, -1)
    args = [x, row(g), w_c, hg_lb, row(hg_norm_g),
            w_d, row(mu_rkv), mu_lora, row(w0), _bf(w1), _bf(w2), row(a0), _bf(a1), _bf(a2),
            _bf(g1), _bf(g2), row(k_k), row(k_a), row(r_k), row(ln_g), row(ln_b), ones]
    in_specs = [pl.BlockSpec((1, tm, D), lambda b, i: (b, i, 0))] + [_resident(t.shape) for t in args[1:]]
    tile = lambda w: pl.BlockSpec((1, tm, w), lambda b, i: (b, i, 0))
    out_specs = [tile(WC), tile(W)]
    out_shape = [jax.ShapeDtypeStruct((B, S, WC), BF16), jax.ShapeDtypeStruct((B, S, W), BF16)]
    if has_vres:
        mu_v, v0, v1, v2 = vres
        extra = [row(mu_v), row(v0), _bf(v1), _bf(v2)]
        args += [v_first] + extra
        in_specs += [tile(W)] + [_resident(t.shape) for t in extra]
    else:
        out_specs.append(tile(W))
        out_shape.append(jax.ShapeDtypeStruct((B, S, W), F32))
    kern = functools.partial(_cd_kernel, tm=tm, layer=layer, has_vres=has_vres)
    res = pl.pallas_call(
        kern,
        grid=(B, S // tm),
        in_specs=in_specs,
        out_specs=out_specs,
        out_shape=out_shape,
        scratch_shapes=[pltpu.VMEM((tm, WC), F32) for _ in range(5)]
        + [pltpu.VMEM((HG_HEADS, HG_DK, HG_DK), F32)]
        + [pltpu.VMEM((tm, W), F32) for _ in range(6)]
        + [pltpu.VMEM((1, D), F32), pltpu.VMEM((1, 3 * W), F32), pltpu.VMEM((RW_HEADS, RW_N, RW_N), F32)],
        compiler_params=_params("parallel", "arbitrary"),
        name="mixers_cd",
    )(*args)
    if has_vres:
        return res[0], res[1], v_first
    return res[0], res[1], res[2]


def _merge_kernel(x_ref, g_ref, a_ref, halo_ref, cw_ref, cb_ref, clg_ref, clb_ref, yb_ref, yc_ref, yd_ref,
                  wg_ref, bg_ref, wb_ref, wo_ref, o_ref, buf_ref, sh_ref, ya_ref, *, tm, seq):
    conv = _conv_pieces(a_ref, halo_ref, cw_ref, cb_ref, clg_ref, clb_ref, buf_ref, sh_ref, ya_ref, tm=tm, seq=seq)
    per_tick = -(-(SUBLANES - 1 + tm // CONV_RB) // 6)

    def tick():
        for _ in range(per_tick):
            next(conv, None)

    x = x_ref[...]
    D = x.shape[1]
    h = _bf(_rms(x, g_ref[...]))
    gate_of = lambda j: _sigmoid(_dot(h, wg_ref[:, j * D:(j + 1) * D]) + bg_ref[:, j * D:(j + 1) * D])
    mixed = None
    for j, y_ref in ((1, yb_ref), (2, yc_ref), (3, yd_ref)):
        gate = gate_of(j)
        tick()
        t = gate * _dot(y_ref[...], wb_ref[j])
        tick()
        mixed = t if mixed is None else mixed + t
    gate = gate_of(0)
    for _ in conv:
        pass
    mixed = mixed + gate * _dot(ya_ref[...], wb_ref[0])
    o_ref[...] = x + _dot(_bf(mixed), wo_ref[...])


def _merge(x2, g, a, conv_w, conv_b, conv_ln_g, conv_ln_b, ys, w_gate, b_gate, w_branch, w_out, seq, tm):
    T, D = x2.shape
    C = BRANCH_W
    tile = lambda w: pl.BlockSpec((tm, w), lambda i: (i, 0))
    hb = tm // CONV_HALO
    kern = functools.partial(_merge_kernel, tm=tm, seq=seq)
    return pl.pallas_call(
        kern,
        grid=(T // tm,),
        in_specs=[tile(D), _resident((1, D)), tile(C),
                  pl.BlockSpec((CONV_HALO, C), lambda i: (jnp.maximum(i * hb - 1, 0), 0)),
                  _resident(conv_w.shape), _resident((1, C)), _resident((1, C)), _resident((1, C))]
        + [tile(C)] * 3
        + [_resident(w_gate.shape), _resident(b_gate.shape), _resident(w_branch.shape), _resident(w_out.shape)],
        out_specs=tile(D),
        out_shape=jax.ShapeDtypeStruct((T, D), F32),
        scratch_shapes=[pltpu.VMEM((tm + CONV_HALO, C), F32), pltpu.VMEM((SUBLANES, tm + CONV_HALO, C), F32),
                        pltpu.VMEM((tm, C), BF16)],
        compiler_params=_params("parallel"),
        name="merge",
    )(x2, g.reshape(1, D), a, a, conv_w, conv_b.reshape(1, C), conv_ln_g.reshape(1, C), conv_ln_b.reshape(1, C),
      *ys, w_gate, b_gate, w_branch, w_out)


def _ffn_kernel(x_ref, g_ref, wg_ref, wu_ref, wd_ref, fg_ref, o_ref, *, hc, final):
    x = x_ref[...]
    h = _bf(_rms(x, g_ref[...]))
    hidden = wg_ref.shape[1]
    acc = x
    for c in range(hidden // hc):
        cs = slice(c * hc, (c + 1) * hc)
        act = _silu(_dot(h, wg_ref[:, cs])) * _dot(h, wu_ref[:, cs])
        acc = acc + _dot(_bf(act), wd_ref[cs, :])
    if final:
        acc = _rms(acc, fg_ref[...])
    o_ref[...] = acc


def _ffn(x2, g, w_gate, w_up, w_down, final_g, final, tm):
    T, D = x2.shape
    hidden = w_gate.shape[1]
    kern = functools.partial(_ffn_kernel, hc=256, final=final)
    return pl.pallas_call(
        kern,
        grid=(T // tm,),
        in_specs=[pl.BlockSpec((tm, D), lambda i: (i, 0)), _resident((1, D)),
                  _resident(w_gate.shape), _resident(w_up.shape), _resident(w_down.shape), _resident((1, D))],
        out_specs=pl.BlockSpec((tm, D), lambda i: (i, 0)),
        out_shape=jax.ShapeDtypeStruct((T, D), F32),
        compiler_params=_params("parallel"),
        name="ffn",
    )(x2, g.reshape(1, D), w_gate, w_up, w_down, final_g.reshape(1, D))


def kernel(x, norm_mix_g, w_in, w_gate, b_gate, conv_w, conv_b, conv_ln_g, conv_ln_b, sgu_ln_g, sgu_ln_b, sgu_w, sgu_b, hg_lb, hg_norm_g, rw_mu_rkv, rw_mu_lora, rw_w0, rw_w1, rw_w2, rw_a0, rw_a1, rw_a2, rw_g1, rw_g2, rw_k_k, rw_k_a, rw_r_k, rw_ln_g, rw_ln_b, rw_mu_vres, rw_v0, rw_v1, rw_v2, w_branch, w_out, norm_ffn_g, w_ffn_gate, w_ffn_up, w_ffn_down, final_norm_g):
    B, S, D = x.shape
    depth = w_in.shape[0]
    T = B * S
    tm = 256
    tm_mm = 512
    n_ab = 4 * BRANCH_W
    n_c = n_ab + 4 * HG_HEADS * HG_DK
    v_first = None
    for l in range(depth):
        w_in_l = _bf(w_in[l])
        x2 = x.reshape(T, D)
        a, y_b = _proj_ab(x2, norm_mix_g[l], w_in_l[:, :n_ab], sgu_ln_g[l], sgu_ln_b[l], sgu_w[l], sgu_b[l], tm_mm)
        vres = None if l == 0 else (rw_mu_vres[l - 1], rw_v0[l - 1], rw_v1[l - 1], rw_v2[l - 1])
        y_c, y_d, v_first = _mixers_cd(x, norm_mix_g[l], w_in_l[:, n_ab:n_c], hg_lb, hg_norm_g[l], l,
                                       w_in_l[:, n_c:], rw_mu_rkv[l], rw_mu_lora[l],
                                       rw_w0[l], rw_w1[l], rw_w2[l], rw_a0[l], rw_a1[l], rw_a2[l],
                                       rw_g1[l], rw_g2[l], rw_k_k[l], rw_k_a[l], rw_r_k[l], rw_ln_g[l], rw_ln_b[l],
                                       v_first, vres, tm)
        wg = _bf(jnp.concatenate([w_gate[l, j] for j in range(4)], axis=1))
        bg = b_gate[l].reshape(1, -1)
        x2 = _merge(x2, norm_mix_g[l], a, conv_w[l], conv_b[l], conv_ln_g[l], conv_ln_b[l],
                    (y_b, y_c.reshape(T, -1), y_d.reshape(T, -1)),
                    wg, bg, _bf(w_branch[l]), _bf(w_out[l]), S, tm_mm)
        x2 = _ffn(x2, norm_ffn_g[l], _bf(w_ffn_gate[l]), _bf(w_ffn_up[l]), _bf(w_ffn_down[l]),
                  final_norm_g, l == depth - 1, tm_mm)
        x = x2.reshape(B, S, D)
    return x
```

```python
import functools

import jax
import jax.numpy as jnp
from jax import lax
from jax.experimental import pallas as pl
from jax.experimental.pallas import tpu as pltpu

F32 = jnp.float32
BF16 = jnp.bfloat16

CHUNK = 64
SUB = 8
BRANCH_W = 512
CONV_WIDTH = 31
SUBLANES = 8
CONV_HALO = 32
SGU_GROUPS = 4
SGU_WIN = 128
HG_HEADS = 4
HG_DK = 128
RW_HEADS = 8
RW_N = 64
RMS_EPS = 1e-6
LN_EPS = 1e-5
RW_GN_EPS = 64e-5
VMEM_LIMIT = 56 * 1024 * 1024


def _bf(x):
    return x.astype(BF16)


def _dot(a, b):
    return jnp.dot(a, b, preferred_element_type=F32)


def _dot_nt(a, b):
    return lax.dot_general(a, b, (((1,), (1,)), ((), ())), preferred_element_type=F32)


def _dot_tn(a, b):
    return lax.dot_general(a, b, (((0,), (0,)), ((), ())), preferred_element_type=F32)


def _split(x, n):
    parts = []
    r = x
    for _ in range(n - 1):
        p = _bf(r)
        parts.append(p)
        r = r - p.astype(F32)
    parts.append(_bf(r))
    return parts


def _mm(a, b, passes, dot=_dot):
    if passes == 1:
        return dot(_bf(a), _bf(b))
    n = 2 if passes == 3 else 3
    ap, bp = _split(a, n), _split(b, n)
    acc = None
    for i in range(n):
        for j in range(n):
            if i + j < n:
                t = dot(ap[i], bp[j])
                acc = t if acc is None else acc + t
    return acc


def _mm_exact_lhs(a_bf, b, n, dot=_dot):
    acc = None
    for p in _split(b, n):
        t = dot(a_bf, p)
        acc = t if acc is None else acc + t
    return acc


def _rms(x, g, eps=RMS_EPS):
    return x * lax.rsqrt(jnp.mean(x * x, axis=-1, keepdims=True) + eps) * g


def _layer_norm(x, g, b, eps=LN_EPS):
    xc = x - jnp.mean(x, axis=-1, keepdims=True)
    return xc * lax.rsqrt(jnp.mean(xc * xc, axis=-1, keepdims=True) + eps) * g + b


def _sigmoid(x):
    return 1.0 / (1.0 + jnp.exp(-x))


def _silu(x):
    return x * _sigmoid(x)


def _gelu(x):
    return 0.5 * x * (1.0 + lax.erf(x * 0.7071067811865476))


def _softplus(x):
    return jnp.maximum(x, 0.0) + jnp.log1p(jnp.exp(-jnp.abs(x)))


def _iota2(shape, dim):
    return lax.broadcasted_iota(jnp.int32, shape, dim)


def _resident(shape):
    nd = len(shape)
    return pl.BlockSpec(shape, lambda *_: (0,) * nd, pipeline_mode=pl.Buffered(1))


def _resident_cols(rows, width, col_block):
    return pl.BlockSpec((rows, width), lambda *_: (0, col_block), pipeline_mode=pl.Buffered(1))


def _params(*sem):
    return pltpu.CompilerParams(dimension_semantics=sem, vmem_limit_bytes=VMEM_LIMIT)


def _ab_kernel(x_ref, g_ref, w_ref, lng_ref, lnb_ref, wsp_ref, bsp_ref, a_ref, yb_ref, *, tm):
    h = _bf(_rms(x_ref[...], g_ref[...]))
    pa = _dot(h, w_ref[:, 0:2 * BRANCH_W])
    a_ref[...] = pa[:, :BRANCH_W] * _sigmoid(pa[:, BRANCH_W:])
    pb = _dot(h, w_ref[:, 2 * BRANCH_W:4 * BRANCH_W])
    u = _gelu(pb[:, :BRANCH_W])
    v = _bf(_layer_norm(_gelu(pb[:, BRANCH_W:]), lng_ref[...], lnb_ref[...]))
    row = _iota2((SGU_WIN, SGU_WIN), 0)
    col = _iota2((SGU_WIN, SGU_WIN), 1)
    allowed = (col // CHUNK) <= (row // CHUNK)
    gw = BRANCH_W // SGU_GROUPS
    for g in range(SGU_GROUPS):
        wg = _bf(jnp.where(allowed, wsp_ref[g], 0.0))
        bg = bsp_ref[g]
        for w in range(tm // SGU_WIN):
            rs = slice(w * SGU_WIN, (w + 1) * SGU_WIN)
            cs = slice(g * gw, (g + 1) * gw)
            mixed = _dot(wg, v[rs, cs]) + bg
            yb_ref[rs, cs] = (u[rs, cs] * mixed).astype(yb_ref.dtype)


def _proj_ab(x2, g, w_in, sgu_ln_g, sgu_ln_b, sgu_w, sgu_b, tm):
    T, D = x2.shape
    kern = functools.partial(_ab_kernel, tm=tm)
    return pl.pallas_call(
        kern,
        grid=(T // tm,),
        in_specs=[
            pl.BlockSpec((tm, D), lambda i: (i, 0)),
            _resident((1, D)),
            _resident_cols(D, 4 * BRANCH_W, 0),
            _resident((1, BRANCH_W)),
            _resident((1, BRANCH_W)),
            _resident(sgu_w.shape),
            _resident((SGU_GROUPS, SGU_WIN, 1)),
        ],
        out_specs=[
            pl.BlockSpec((tm, BRANCH_W), lambda i: (i, 0)),
            pl.BlockSpec((tm, BRANCH_W), lambda i: (i, 0)),
        ],
        out_shape=[
            jax.ShapeDtypeStruct((T, BRANCH_W), F32),
            jax.ShapeDtypeStruct((T, BRANCH_W), BF16),
        ],
        compiler_params=_params("parallel"),
        name="proj_ab",
    )(x2, g.reshape(1, D), w_in, sgu_ln_g.reshape(1, -1), sgu_ln_b.reshape(1, -1), sgu_w,
      sgu_b.reshape(SGU_GROUPS, SGU_WIN, 1))


CONV_RB = 128


def _conv_pieces(a_ref, halo_ref, w_ref, b_ref, lng_ref, lnb_ref, buf_ref, sh_ref, ya_ref, *, tm, seq):
    first = (pl.program_id(0) * tm) % seq == 0
    buf_ref[0:CONV_HALO, :] = jnp.where(first, 0.0, halo_ref[...])
    buf_ref[CONV_HALO:, :] = a_ref[...]
    n_sh = tm + CONV_HALO - SUBLANES
    for s in range(1, SUBLANES):
        for r0 in range(0, n_sh, CONV_RB):
            nr = min(CONV_RB, n_sh - r0)
            sh_ref[s, r0:r0 + nr, :] = buf_ref[pl.ds(r0 + s, nr), :]
        yield
    off = CONV_HALO - (CONV_WIDTH - 1)
    for r in range(tm // CONV_RB):
        acc = None
        for j in range(CONV_WIDTH):
            q, s = divmod(off + j, SUBLANES)
            rows = pl.ds(r * CONV_RB + q * SUBLANES, CONV_RB)
            tap = buf_ref[rows, :] if s == 0 else sh_ref[s, rows, :]
            t = tap.reshape(CONV_RB // SUBLANES, SUBLANES, BRANCH_W) * w_ref[j]
            acc = t if acc is None else acc + t
        acc = acc.reshape(CONV_RB, BRANCH_W) + b_ref[...]
        y = _layer_norm(acc, lng_ref[...], lnb_ref[...])
        ya_ref[r * CONV_RB:(r + 1) * CONV_RB, :] = _silu(y).astype(ya_ref.dtype)
        yield


RW_PASSES = 1
HG_STEPS_PER_TICK = 4


def _hg_pieces(h, w_ref, lb_ref, ng_ref, o_ref, q_s, lf_s, k_s, v_s, og_s, st_ref, *, tm, layer):
    W = HG_HEADS * HG_DK
    p = _dot(h, w_ref[...])
    z = p[:, W:2 * W]
    lbp = lb_ref[...]
    e = jnp.exp(lbp - jnp.max(lbp, axis=0, keepdims=True))
    sm = e / jnp.sum(e, axis=0, keepdims=True)
    cs0 = sm[0:1]
    cs = cs0
    for l in range(1, layer + 1):
        cs = cs + sm[l:l + 1]
    lb = cs - cs0
    log_sig = -_softplus(-z)
    t1 = jnp.log(lb)
    t2 = jnp.log1p(-lb) + log_sig
    log_f = jnp.maximum(t1, t2) + jnp.log1p(jnp.exp(-jnp.abs(t1 - t2)))
    q_s[...] = _silu(p[:, 0:W])
    lf_s[...] = log_f
    k_s[...] = (1.0 - lb) * _sigmoid(-z)
    v_s[...] = p[:, 2 * W:3 * W]
    og_s[...] = _silu(p[:, 3 * W:4 * W]) * ng_ref[...]
    yield

    tri = _bf((_iota2((CHUNK, CHUNK), 1) <= _iota2((CHUNK, CHUNK), 0)).astype(F32))
    nsub = CHUNK // SUB
    rowc = _iota2((CHUNK, W), 0)
    row_s = _iota2((SUB, CHUNK), 0)
    lane_c = _iota2((SUB, CHUNK), 1)
    place = [(lane_c == c) & (row_s >= c % SUB) for c in range(CHUNK)]
    H = range(HG_HEADS)
    hs = lambda a, hd: a[:, hd * HG_DK:(hd + 1) * HG_DK]
    for c in range(tm // CHUNK):
        rows = pl.ds(c * CHUNK, CHUNK)
        q = q_s[rows, :]
        k = k_s[rows, :]
        v = _bf(v_s[rows, :])
        b = _mm_exact_lhs(tri, lf_s[rows, :], 3)
        bend = jnp.concatenate(
            [jnp.broadcast_to(b[(j + 1) * SUB - 1:(j + 1) * SUB, :], (SUB, W)) for j in range(nsub)], axis=0)
        khat = k * jnp.exp(bend - b)
        b_end = b[CHUNK - 1:CHUNK, :]
        q_in = _bf(q * jnp.exp(b))
        k_out = _bf(k * jnp.exp(b_end - b))
        yield
        st = [st_ref[hd] for hd in H]
        from_state = [_dot_nt(hs(q_in, hd), _bf(st[hd])) for hd in H]
        for hd in H:
            st_ref[hd] = st[hd] * hs(jnp.exp(b_end), hd) + _dot_tn(hs(v, hd), hs(k_out, hd))
        yield
        inter = []
        for i in range(1, nsub):
            rs = slice(i * SUB, (i + 1) * SUB)
            beta = b[i * SUB - 1:i * SUB, :]
            qt = _bf(q[rs] * jnp.exp(b[rs] - beta))
            rhs = _bf(jnp.where(rowc < i * SUB, khat * jnp.exp(beta - bend), 0.0))
            inter.append([_dot_nt(hs(qt, hd), hs(rhs, hd)) for hd in H])
            yield
        outs = []
        for hd in H:
            blocks = []
            for i in range(nsub):
                rs = slice(i * SUB, (i + 1) * SUB)
                bi = hs(b, hd)[rs]
                qi = hs(q, hd)[rs]
                ki = hs(k, hd)[rs]
                sc = jnp.zeros((SUB, CHUNK), F32) if i == 0 else inter[i - 1][hd]
                for s in range(SUB):
                    ee = jnp.exp(bi - bi[s:s + 1, :])
                    col = jnp.sum(qi * ki[s:s + 1, :] * ee, axis=1, keepdims=True)
                    sc = jnp.where(place[i * SUB + s], col, sc)
                blocks.append(sc)
                yield
            scores = jnp.concatenate(blocks, axis=0)
            out = _dot(_bf(scores), hs(v, hd)) + from_state[hd]
            ms = jnp.mean(out * out, axis=1, keepdims=True)
            outs.append(out * lax.rsqrt(ms + RMS_EPS))
        o_ref[0, rows, :] = (jnp.concatenate(outs, axis=1) * og_s[rows, :]).astype(o_ref.dtype)
        yield


def _rw_chunks(chunks, h0s, tri, masks, tick):
    strict, incl, same16, same32, eye = masks
    mm = functools.partial(_mm, passes=RW_PASSES)
    mm_nt = functools.partial(_mm, passes=RW_PASSES, dot=_dot_nt)
    mm_tn = functools.partial(_mm, passes=RW_PASSES, dot=_dot_tn)
    pre = []
    for (r, lw, k, v, kk, bv) in chunks:
        c = _mm_exact_lhs(tri, lw, 3)
        c_last = c[CHUNK - 1:CHUNK, :]
        e_neg = jnp.exp(-c)
        e_end = jnp.exp(c_last - c)
        pre.append(dict(rt=r * jnp.exp(c), p=kk * jnp.exp(c - lw), kin=k * e_neg, bin=bv * e_neg,
                        kg=k * e_end, bg=bv * e_end, gd=jnp.exp(c_last), v=v))
    P = [(ci, h) for ci in range(len(chunks)) for h in range(RW_HEADS)]
    sl = lambda name, q: pre[q[0]][name][:, q[1] * RW_N:(q[1] + 1) * RW_N]

    def each(f):
        out = {q: f(q) for q in P}
        tick()
        return out

    lhs = each(lambda q: jnp.concatenate([sl("p", q), sl("rt", q)], axis=0))
    ab = each(lambda q: mm_nt(lhs[q], sl("bin", q)))
    ak = each(lambda q: mm_nt(lhs[q], sl("kin", q)))
    l_raw = each(lambda q: ab[q][:CHUNK])
    m = each(lambda q: jnp.where(strict, ak[q][:CHUNK], 0.0))
    arb = each(lambda q: jnp.where(incl, ab[q][CHUNK:], 0.0))
    ark = each(lambda q: jnp.where(incl, ak[q][CHUNK:], 0.0))
    n1 = each(lambda q: jnp.where(strict & same16, -l_raw[q], 0.0))
    rows2 = lambda a, b: jnp.concatenate([a, b], axis=0)
    n2 = each(lambda q: mm(n1[q], n1[q]))
    a2 = each(lambda q: mm(rows2(n1[q], n2[q]), n2[q]))
    s2 = each(lambda q: eye + n1[q] + n2[q] + a2[q][:CHUNK])
    n4 = each(lambda q: a2[q][CHUNK:])
    a4 = each(lambda q: mm(rows2(s2[q], n4[q]), n4[q]))
    s4 = each(lambda q: s2[q] + a4[q][:CHUNK])
    n8 = each(lambda q: a4[q][CHUNK:])
    t16 = each(lambda q: s4[q] + mm(s4[q], n8[q]))
    lo1 = each(lambda q: jnp.where(strict & same32 & jnp.logical_not(same16), l_raw[q], 0.0))
    x1 = each(lambda q: mm(t16[q], lo1[q]))
    t32 = each(lambda q: t16[q] - mm(x1[q], t16[q]))
    lo2 = each(lambda q: jnp.where(strict & jnp.logical_not(same32), l_raw[q], 0.0))
    x2 = each(lambda q: mm(t32[q], lo2[q]))
    t = each(lambda q: t32[q] - mm(x2[q], t32[q]))
    mv = each(lambda q: mm(m[q], sl("v", q)))
    z = each(lambda q: mm(t[q], jnp.concatenate([sl("p", q), mv[q]], axis=1)))
    az = each(lambda q: mm(arb[q], z[q]))
    bz = each(lambda q: mm_tn(sl("bg", q), z[q]))
    rq = each(lambda q: sl("rt", q) - az[q][:, :RW_N])
    yv = each(lambda q: mm(ark[q], sl("v", q)) - az[q][:, RW_N:])
    g = each(lambda q: jnp.where(eye > 0.0, jnp.broadcast_to(sl("gd", q), (RW_N, RW_N)), 0.0) - bz[q][:, :RW_N])
    hv = each(lambda q: mm_tn(sl("kg", q), sl("v", q)) - bz[q][:, RW_N:])
    hs = list(h0s)
    ys = []
    for ci in range(len(chunks)):
        yh = [mm(rows2(rq[(ci, h)], g[(ci, h)]), hs[h]) for h in range(RW_HEADS)]
        ys.append(jnp.concatenate([yh[h][:CHUNK] + yv[(ci, h)] for h in range(RW_HEADS)], axis=1))
        hs = [yh[h][CHUNK:] + hv[(ci, h)] for h in range(RW_HEADS)]
    return ys, hs


def _head_sums(x, ones_half):
    hw = ones_half.shape[0]
    return jnp.concatenate([_dot(_bf(x[:, j * hw:(j + 1) * hw]), ones_half) for j in range(x.shape[1] // hw)], axis=1)


def _cd_kernel(*refs, tm, layer, has_vres):
    (x_ref, g_ref, wc_ref, lb_ref, ng_ref, w_ref, mur_ref, mul_ref, w0_ref, w1_ref, w2_ref, a0_ref, a1_ref, a2_ref,
     g1_ref, g2_ref, kk_ref, ka_ref, rk_ref, lng_ref, lnb_ref, ones_ref) = refs[:22]
    if has_vres:
        vf_ref, muv_ref, v0_ref, v1_ref, v2_ref = refs[22:27]
        oc_ref, o_ref = refs[27:29]
        scratch = refs[29:]
    else:
        oc_ref, o_ref, vf_out_ref = refs[22:25]
        scratch = refs[25:]
    (hq_s, hlf_s, hk_s, hv_s, hog_s, hst_ref, r_s, lw_s, k_s, v_s, kk_s, bv_s, ch_s, cp_s, st_ref) = scratch
    W = RW_HEADS * RW_N

    @pl.when(pl.program_id(1) == 0)
    def _():
        hst_ref[...] = jnp.zeros_like(hst_ref)
        st_ref[...] = jnp.zeros_like(st_ref)
        ch_s[...] = jnp.zeros_like(ch_s)
        cp_s[...] = jnp.zeros_like(cp_s)

    h = _rms(x_ref[0], g_ref[...])
    hb = _bf(h)
    p = _dot(hb, w_ref[...])
    row_d = _iota2(h.shape, 0)
    row_p = _iota2(p.shape, 0)
    hs = jnp.where(row_d == 0, ch_s[...], pltpu.roll(h, 1, 0))
    ps = jnp.where(row_p == 0, cp_s[...], pltpu.roll(p, 1, 0))
    ch_s[...] = h[tm - 1:tm, :]
    cp_s[...] = p[tm - 1:tm, :]
    dh = hs - h
    pm = p + (ps - p) * mur_ref[...]
    r = pm[:, 0:W]
    k = pm[:, W:2 * W]
    v = pm[:, 2 * W:3 * W]
    xw = _bf(h + dh * mul_ref[0:1, :])
    xa = _bf(h + dh * mul_ref[1:2, :])
    xg = _bf(h + dh * mul_ref[2:3, :])
    wl = w0_ref[...] + _dot(_bf(jnp.tanh(_dot(xw, w1_ref[...]))), w2_ref[...])
    w_log = -_softplus(-wl) - 0.5
    lw = -jnp.exp(w_log)
    if has_vres:
        xv = _bf(h + dh * muv_ref[...])
        mixv = _sigmoid(v0_ref[...] + _dot(_bf(_dot(xv, v1_ref[...])), v2_ref[...]))
        v = v + (vf_ref[0] - v) * mixv
    else:
        vf_out_ref[0] = v
    a = _sigmoid(a0_ref[...] + _dot(_bf(_dot(xa, a1_ref[...])), a2_ref[...]))
    gate = _dot(_bf(_sigmoid(_dot(xg, g1_ref[...]))), g2_ref[...])
    ones = ones_ref[...]
    kk = k * kk_ref[...]
    ssq = _head_sums(kk * kk, ones)
    kk = kk * lax.rsqrt(jnp.maximum(ssq, 1e-24))
    k = k * (1.0 + (a - 1.0) * ka_ref[...])
    r_s[...] = r
    lw_s[...] = lw
    k_s[...] = k
    v_s[...] = v
    kk_s[...] = kk
    bv_s[...] = kk * a

    hg = _hg_pieces(hb, wc_ref, lb_ref, ng_ref, oc_ref, hq_s, hlf_s, hk_s, hv_s, hog_s, hst_ref, tm=tm, layer=layer)
    next(hg)

    def tick():
        for _ in range(HG_STEPS_PER_TICK):
            next(hg, None)

    ri = _iota2((CHUNK, CHUNK), 0)
    ci = _iota2((CHUNK, CHUNK), 1)
    strict = ci < ri
    incl = ci <= ri
    same16 = (ri // 16) == (ci // 16)
    same32 = (ri // 32) == (ci // 32)
    eye = (ri == ci).astype(F32)
    masks = (strict, incl, same16, same32, eye)
    tri = _bf(incl.astype(F32))
    rows = [pl.ds(u * CHUNK, CHUNK) for u in range(tm // CHUNK)]
    ys, h1 = _rw_chunks([(r_s[rw, :], lw_s[rw, :], k_s[rw, :], v_s[rw, :], kk_s[rw, :], bv_s[rw, :]) for rw in rows],
                        [st_ref[hd] for hd in range(RW_HEADS)], tri, masks, tick)
    for hd in range(RW_HEADS):
        st_ref[hd] = h1[hd]
    for _ in hg:
        pass

    y = jnp.concatenate(ys, axis=0)
    inv_n = 1.0 / RW_N
    y_hi = _bf(y).astype(F32)
    mean = (_head_sums(y_hi, ones) + _head_sums(y - y_hi, ones)) * inv_n
    yc = y - mean
    var = _head_sums(yc * yc, ones) * inv_n
    yn = yc * lax.rsqrt(var + RW_GN_EPS) * lng_ref[...] + lnb_ref[...]
    bonus = _head_sums(r * k * rk_ref[...], ones)
    yn = yn + bonus * v
    o_ref[0] = (yn * gate).astype(o_ref.dtype)


def _mixers_cd(x, g, w_in, hg_lb, hg_norm_g, layer, w_d, mu_rkv, mu_lora, w0, w1, w2, a0, a1, a2, g1, g2,
               k_k, k_a, r_k, ln_g, ln_b, v_first, vres, tm):
    B, S, D = x.shape
    W = RW_HEADS * RW_N
    WC = HG_HEADS * HG_DK
    has_vres = vres is not None
    hid = jnp.arange(W // 2) // RW_N
    ones = (hid[:, None] == hid[None, :]).astype(BF16)
    row = lambda t: t.reshape(1, -1)
    args = [x, row(g), w_in, hg_lb, row(hg_norm_g),
            w_d, row(mu_rkv), mu_lora, row(w0), _bf(w1), _bf(w2), row(a0), _bf(a1), _bf(a2),
            _bf(g1), _bf(g2), row(k_k), row(k_a), row(r_k), row(ln_g), row(ln_b), ones]
    in_specs = [pl.BlockSpec((1, tm, D), lambda b, i: (b, i, 0))] + [_resident(t.shape) for t in args[1:]]
    in_specs[2] = _resident_cols(D, 4 * WC, 1)
    tile = lambda w: pl.BlockSpec((1, tm, w), lambda b, i: (b, i, 0))
    out_specs = [tile(WC), tile(W)]
    out_shape = [jax.ShapeDtypeStruct((B, S, WC), BF16), jax.ShapeDtypeStruct((B, S, W), BF16)]
    if has_vres:
        mu_v, v0, v1, v2 = vres
        extra = [row(mu_v), row(v0), _bf(v1), _bf(v2)]
        args += [v_first] + extra
        in_specs += [tile(W)] + [_resident(t.shape) for t in extra]
    else:
        out_specs.append(tile(W))
        out_shape.append(jax.ShapeDtypeStruct((B, S, W), F32))
    kern = functools.partial(_cd_kernel, tm=tm, layer=layer, has_vres=has_vres)
    res = pl.pallas_call(
        kern,
        grid=(B, S // tm),
        in_specs=in_specs,
        out_specs=out_specs,
        out_shape=out_shape,
        scratch_shapes=[pltpu.VMEM((tm, WC), F32) for _ in range(5)]
        + [pltpu.VMEM((HG_HEADS, HG_DK, HG_DK), F32)]
        + [pltpu.VMEM((tm, W), F32) for _ in range(6)]
        + [pltpu.VMEM((1, D), F32), pltpu.VMEM((1, 3 * W), F32), pltpu.VMEM((RW_HEADS, RW_N, RW_N), F32)],
        compiler_params=_params("parallel", "arbitrary"),
        name="mixers_cd",
    )(*args)
    if has_vres:
        return res[0], res[1], v_first
    return res[0], res[1], res[2]


def _merge_kernel(x_ref, g_ref, a_ref, halo_ref, cw_ref, cb_ref, clg_ref, clb_ref, yb_ref, yc_ref, yd_ref,
                  wg_ref, bg_ref, wb_ref, wo_ref, o_ref, buf_ref, sh_ref, ya_ref, *, tm, seq):
    conv = _conv_pieces(a_ref, halo_ref, cw_ref, cb_ref, clg_ref, clb_ref, buf_ref, sh_ref, ya_ref, tm=tm, seq=seq)
    per_tick = -(-(SUBLANES - 1 + tm // CONV_RB) // 6)

    def tick():
        for _ in range(per_tick):
            next(conv, None)

    x = x_ref[...]
    h = _bf(_rms(x, g_ref[...]))
    gate_of = lambda j: _sigmoid(_dot(h, wg_ref[j]) + bg_ref[j])
    mixed = None
    for j, y_ref in ((1, yb_ref), (2, yc_ref), (3, yd_ref)):
        gate = gate_of(j)
        tick()
        t = gate * _dot(y_ref[...], wb_ref[j])
        tick()
        mixed = t if mixed is None else mixed + t
    gate = gate_of(0)
    for _ in conv:
        pass
    mixed = mixed + gate * _dot(ya_ref[...], wb_ref[0])
    o_ref[...] = x + _dot(_bf(mixed), wo_ref[...])


def _merge(x2, g, a, conv_w, conv_b, conv_ln_g, conv_ln_b, ys, w_gate, b_gate, w_branch, w_out, seq, tm):
    T, D = x2.shape
    C = BRANCH_W
    tile = lambda w: pl.BlockSpec((tm, w), lambda i: (i, 0))
    hb = tm // CONV_HALO
    kern = functools.partial(_merge_kernel, tm=tm, seq=seq)
    return pl.pallas_call(
        kern,
        grid=(T // tm,),
        in_specs=[tile(D), _resident((1, D)), tile(C),
                  pl.BlockSpec((CONV_HALO, C), lambda i: (jnp.maximum(i * hb - 1, 0), 0)),
                  _resident((CONV_WIDTH, SUBLANES, C)), _resident((1, C)), _resident((1, C)), _resident((1, C))]
        + [tile(C)] * 3
        + [_resident(w_gate.shape), _resident(b_gate.shape), _resident(w_branch.shape), _resident(w_out.shape)],
        out_specs=tile(D),
        out_shape=jax.ShapeDtypeStruct((T, D), F32),
        scratch_shapes=[pltpu.VMEM((tm + CONV_HALO, C), F32), pltpu.VMEM((SUBLANES, tm + CONV_HALO, C), F32),
                        pltpu.VMEM((tm, C), BF16)],
        compiler_params=_params("parallel"),
        name="merge",
    )(x2, g.reshape(1, D), a, a, jnp.broadcast_to(conv_w[:, None, :], (CONV_WIDTH, SUBLANES, C)), conv_b.reshape(1, C), conv_ln_g.reshape(1, C), conv_ln_b.reshape(1, C),
      *ys, w_gate, b_gate, w_branch, w_out)


def _ffn_kernel(x_ref, g_ref, wg_ref, wu_ref, wd_ref, fg_ref, o_ref, *, hc, final):
    x = x_ref[...]
    h = _bf(_rms(x, g_ref[...]))
    hidden = wg_ref.shape[1]
    acc = x
    for c in range(hidden // hc):
        cs = slice(c * hc, (c + 1) * hc)
        act = _silu(_dot(h, wg_ref[:, cs])) * _dot(h, wu_ref[:, cs])
        acc = acc + _dot(_bf(act), wd_ref[cs, :])
    if final:
        acc = _rms(acc, fg_ref[...])
    o_ref[...] = acc


def _ffn(x2, g, w_gate, w_up, w_down, final_g, final, tm):
    T, D = x2.shape
    hidden = w_gate.shape[1]
    kern = functools.partial(_ffn_kernel, hc=256, final=final)
    return pl.pallas_call(
        kern,
        grid=(T // tm,),
        in_specs=[pl.BlockSpec((tm, D), lambda i: (i, 0)), _resident((1, D)),
                  _resident(w_gate.shape), _resident(w_up.shape), _resident(w_down.shape), _resident((1, D))],
        out_specs=pl.BlockSpec((tm, D), lambda i: (i, 0)),
        out_shape=jax.ShapeDtypeStruct((T, D), F32),
        compiler_params=_params("parallel"),
        name="ffn",
    )(x2, g.reshape(1, D), w_gate, w_up, w_down, final_g.reshape(1, D))


def kernel(x, norm_mix_g, w_in, w_gate, b_gate, conv_w, conv_b, conv_ln_g, conv_ln_b, sgu_ln_g, sgu_ln_b, sgu_w, sgu_b, hg_lb, hg_norm_g, rw_mu_rkv, rw_mu_lora, rw_w0, rw_w1, rw_w2, rw_a0, rw_a1, rw_a2, rw_g1, rw_g2, rw_k_k, rw_k_a, rw_r_k, rw_ln_g, rw_ln_b, rw_mu_vres, rw_v0, rw_v1, rw_v2, w_branch, w_out, norm_ffn_g, w_ffn_gate, w_ffn_up, w_ffn_down, final_norm_g):
    B, S, D = x.shape
    depth = w_in.shape[0]
    T = B * S
    tm = 256
    tm_mm = 512
    n_c = 4 * BRANCH_W + 4 * HG_HEADS * HG_DK
    v_first = None
    for l in range(depth):
        w_in_l = _bf(w_in[l])
        x2 = x.reshape(T, D)
        a, y_b = _proj_ab(x2, norm_mix_g[l], w_in_l, sgu_ln_g[l], sgu_ln_b[l], sgu_w[l], sgu_b[l], tm_mm)
        vres = None if l == 0 else (rw_mu_vres[l - 1], rw_v0[l - 1], rw_v1[l - 1], rw_v2[l - 1])
        y_c, y_d, v_first = _mixers_cd(x, norm_mix_g[l], w_in_l, hg_lb, hg_norm_g[l], l,
                                       w_in_l[:, n_c:], rw_mu_rkv[l], rw_mu_lora[l],
                                       rw_w0[l], rw_w1[l], rw_w2[l], rw_a0[l], rw_a1[l], rw_a2[l],
                                       rw_g1[l], rw_g2[l], rw_k_k[l], rw_k_a[l], rw_r_k[l], rw_ln_g[l], rw_ln_b[l],
                                       v_first, vres, tm)
        x2 = _merge(x2, norm_mix_g[l], a, conv_w[l], conv_b[l], conv_ln_g[l], conv_ln_b[l],
                    (y_b, y_c.reshape(T, -1), y_d.reshape(T, -1)),
                    _bf(w_gate[l]), b_gate[l][:, None, :], _bf(w_branch[l]), _bf(w_out[l]), S, tm_mm)
        x2 = _ffn(x2, norm_ffn_g[l], _bf(w_ffn_gate[l]), _bf(w_ffn_up[l]), _bf(w_ffn_down[l]),
                  final_norm_g, l == depth - 1, tm_mm)
        x = x2.reshape(B, S, D)
    return x
```

```python
import functools

import jax
import jax.numpy as jnp
from jax import lax
from jax.experimental import pallas as pl
from jax.experimental.pallas import tpu as pltpu

F32 = jnp.float32
BF16 = jnp.bfloat16

CHUNK = 64
SUB = 8
BRANCH_W = 512
CONV_WIDTH = 31
SUBLANES = 8
CONV_HALO = 32
SGU_GROUPS = 4
SGU_WIN = 128
HG_HEADS = 4
HG_DK = 128
RW_HEADS = 8
RW_N = 64
RMS_EPS = 1e-6
LN_EPS = 1e-5
RW_GN_EPS = 64e-5
VMEM_LIMIT = 56 * 1024 * 1024


def _bf(x):
    return x.astype(BF16)


def _dot(a, b):
    return jnp.dot(a, b, preferred_element_type=F32)


def _dot_nt(a, b):
    return lax.dot_general(a, b, (((1,), (1,)), ((), ())), preferred_element_type=F32)


def _dot_tn(a, b):
    return lax.dot_general(a, b, (((0,), (0,)), ((), ())), preferred_element_type=F32)


def _split(x, n):
    parts = []
    r = x
    for _ in range(n - 1):
        p = _bf(r)
        parts.append(p)
        r = r - p.astype(F32)
    parts.append(_bf(r))
    return parts


def _mm(a, b, passes, dot=_dot):
    if passes == 1:
        return dot(_bf(a), _bf(b))
    n = 2 if passes == 3 else 3
    ap, bp = _split(a, n), _split(b, n)
    acc = None
    for i in range(n):
        for j in range(n):
            if i + j < n:
                t = dot(ap[i], bp[j])
                acc = t if acc is None else acc + t
    return acc


def _mm_exact_lhs(a_bf, b, n, dot=_dot):
    acc = None
    for p in _split(b, n):
        t = dot(a_bf, p)
        acc = t if acc is None else acc + t
    return acc


def _rms(x, g, eps=RMS_EPS):
    return x * lax.rsqrt(jnp.mean(x * x, axis=-1, keepdims=True) + eps) * g


def _layer_norm(x, g, b, eps=LN_EPS):
    xc = x - jnp.mean(x, axis=-1, keepdims=True)
    return xc * lax.rsqrt(jnp.mean(xc * xc, axis=-1, keepdims=True) + eps) * g + b


def _sigmoid(x):
    return 1.0 / (1.0 + jnp.exp(-x))


def _silu(x):
    return x * _sigmoid(x)


def _gelu(x):
    return 0.5 * x * (1.0 + lax.erf(x * 0.7071067811865476))


def _softplus(x):
    return jnp.maximum(x, 0.0) + jnp.log1p(jnp.exp(-jnp.abs(x)))


def _iota2(shape, dim):
    return lax.broadcasted_iota(jnp.int32, shape, dim)


def _resident(shape):
    nd = len(shape)
    return pl.BlockSpec(shape, lambda *_: (0,) * nd, pipeline_mode=pl.Buffered(1))


def _resident_cols(rows, width, col_block):
    return pl.BlockSpec((rows, width), lambda *_: (0, col_block), pipeline_mode=pl.Buffered(1))


def _params(*sem):
    return pltpu.CompilerParams(dimension_semantics=sem, vmem_limit_bytes=VMEM_LIMIT)


CONV_RB = 128


def _conv_pieces(w_ref, b_ref, lng_ref, lnb_ref, buf_ref, sh_ref, ya_ref, *, tm):
    n_sh = tm + CONV_HALO - SUBLANES
    for s in range(1, SUBLANES):
        for r0 in range(0, n_sh, CONV_RB):
            nr = min(CONV_RB, n_sh - r0)
            sh_ref[s, r0:r0 + nr, :] = buf_ref[pl.ds(r0 + s, nr), :]
        yield
    off = CONV_HALO - (CONV_WIDTH - 1)
    for r in range(tm // CONV_RB):
        acc = None
        for j in range(CONV_WIDTH):
            q, s = divmod(off + j, SUBLANES)
            rows = pl.ds(r * CONV_RB + q * SUBLANES, CONV_RB)
            tap = buf_ref[rows, :] if s == 0 else sh_ref[s, rows, :]
            t = tap.reshape(CONV_RB // SUBLANES, SUBLANES, BRANCH_W) * w_ref[j]
            acc = t if acc is None else acc + t
        acc = acc.reshape(CONV_RB, BRANCH_W) + b_ref[...]
        y = _layer_norm(acc, lng_ref[...], lnb_ref[...])
        ya_ref[r * CONV_RB:(r + 1) * CONV_RB, :] = _silu(y).astype(ya_ref.dtype)
        yield


RW_PASSES = 1
HG_STEPS_PER_TICK = 4


def _hg_pieces(h, w_ref, lb_ref, ng_ref, o_ref, q_s, lf_s, k_s, v_s, og_s, st_ref, *, tm, layer):
    W = HG_HEADS * HG_DK
    p = _dot(h, w_ref[...])
    z = p[:, W:2 * W]
    lbp = lb_ref[...]
    e = jnp.exp(lbp - jnp.max(lbp, axis=0, keepdims=True))
    sm = e / jnp.sum(e, axis=0, keepdims=True)
    cs0 = sm[0:1]
    cs = cs0
    for l in range(1, layer + 1):
        cs = cs + sm[l:l + 1]
    lb = cs - cs0
    log_sig = -_softplus(-z)
    t1 = jnp.log(lb)
    t2 = jnp.log1p(-lb) + log_sig
    log_f = jnp.maximum(t1, t2) + jnp.log1p(jnp.exp(-jnp.abs(t1 - t2)))
    q_s[...] = _silu(p[:, 0:W])
    lf_s[...] = log_f
    k_s[...] = (1.0 - lb) * _sigmoid(-z)
    v_s[...] = p[:, 2 * W:3 * W]
    og_s[...] = _silu(p[:, 3 * W:4 * W]) * ng_ref[...]
    yield

    tri = _bf((_iota2((CHUNK, CHUNK), 1) <= _iota2((CHUNK, CHUNK), 0)).astype(F32))
    nsub = CHUNK // SUB
    rowc = _iota2((CHUNK, W), 0)
    row_s = _iota2((SUB, CHUNK), 0)
    lane_c = _iota2((SUB, CHUNK), 1)
    place = [(lane_c == c) & (row_s >= c % SUB) for c in range(CHUNK)]
    H = range(HG_HEADS)
    hs = lambda a, hd: a[:, hd * HG_DK:(hd + 1) * HG_DK]
    for c in range(tm // CHUNK):
        rows = pl.ds(c * CHUNK, CHUNK)
        q = q_s[rows, :]
        k = k_s[rows, :]
        v = _bf(v_s[rows, :])
        b = _mm_exact_lhs(tri, lf_s[rows, :], 3)
        bend = jnp.concatenate(
            [jnp.broadcast_to(b[(j + 1) * SUB - 1:(j + 1) * SUB, :], (SUB, W)) for j in range(nsub)], axis=0)
        khat = k * jnp.exp(bend - b)
        b_end = b[CHUNK - 1:CHUNK, :]
        q_in = _bf(q * jnp.exp(b))
        k_out = _bf(k * jnp.exp(b_end - b))
        yield
        st = [st_ref[hd] for hd in H]
        from_state = [_dot_nt(hs(q_in, hd), _bf(st[hd])) for hd in H]
        for hd in H:
            st_ref[hd] = st[hd] * hs(jnp.exp(b_end), hd) + _dot_tn(hs(v, hd), hs(k_out, hd))
        yield
        inter = []
        for i in range(1, nsub):
            rs = slice(i * SUB, (i + 1) * SUB)
            beta = b[i * SUB - 1:i * SUB, :]
            qt = _bf(q[rs] * jnp.exp(b[rs] - beta))
            rhs = _bf(jnp.where(rowc < i * SUB, khat * jnp.exp(beta - bend), 0.0))
            inter.append([_dot_nt(hs(qt, hd), hs(rhs, hd)) for hd in H])
            yield
        outs = []
        for hd in H:
            blocks = []
            for i in range(nsub):
                rs = slice(i * SUB, (i + 1) * SUB)
                bi = hs(b, hd)[rs]
                qi = hs(q, hd)[rs]
                ki = hs(k, hd)[rs]
                sc = jnp.zeros((SUB, CHUNK), F32) if i == 0 else inter[i - 1][hd]
                for s in range(SUB):
                    ee = jnp.exp(bi - bi[s:s + 1, :])
                    col = jnp.sum(qi * ki[s:s + 1, :] * ee, axis=1, keepdims=True)
                    sc = jnp.where(place[i * SUB + s], col, sc)
                blocks.append(sc)
                yield
            scores = jnp.concatenate(blocks, axis=0)
            out = _dot(_bf(scores), hs(v, hd)) + from_state[hd]
            ms = jnp.mean(out * out, axis=1, keepdims=True)
            outs.append(out * lax.rsqrt(ms + RMS_EPS))
        o_ref[0, rows, :] = (jnp.concatenate(outs, axis=1) * og_s[rows, :]).astype(o_ref.dtype)
        yield


def _rw_chunks(chunks, h0s, tri, masks, tick):
    strict, incl, same16, same32, eye = masks
    mm = functools.partial(_mm, passes=RW_PASSES)
    mm_nt = functools.partial(_mm, passes=RW_PASSES, dot=_dot_nt)
    mm_tn = functools.partial(_mm, passes=RW_PASSES, dot=_dot_tn)
    pre = []
    for (r, lw, k, v, kk, bv) in chunks:
        c = _mm_exact_lhs(tri, lw, 3)
        c_last = c[CHUNK - 1:CHUNK, :]
        e_neg = jnp.exp(-c)
        e_end = jnp.exp(c_last - c)
        pre.append(dict(rt=r * jnp.exp(c), p=kk * jnp.exp(c - lw), kin=k * e_neg, bin=bv * e_neg,
                        kg=k * e_end, bg=bv * e_end, gd=jnp.exp(c_last), v=v))
    P = [(ci, h) for ci in range(len(chunks)) for h in range(RW_HEADS)]
    sl = lambda name, q: pre[q[0]][name][:, q[1] * RW_N:(q[1] + 1) * RW_N]

    def each(f):
        out = {q: f(q) for q in P}
        tick()
        return out

    lhs = each(lambda q: jnp.concatenate([sl("p", q), sl("rt", q)], axis=0))
    ab = each(lambda q: mm_nt(lhs[q], sl("bin", q)))
    ak = each(lambda q: mm_nt(lhs[q], sl("kin", q)))
    l_raw = each(lambda q: ab[q][:CHUNK])
    m = each(lambda q: jnp.where(strict, ak[q][:CHUNK], 0.0))
    arb = each(lambda q: jnp.where(incl, ab[q][CHUNK:], 0.0))
    ark = each(lambda q: jnp.where(incl, ak[q][CHUNK:], 0.0))
    n1 = each(lambda q: jnp.where(strict & same16, -l_raw[q], 0.0))
    rows2 = lambda a, b: jnp.concatenate([a, b], axis=0)
    n2 = each(lambda q: mm(n1[q], n1[q]))
    a2 = each(lambda q: mm(rows2(n1[q], n2[q]), n2[q]))
    s2 = each(lambda q: eye + n1[q] + n2[q] + a2[q][:CHUNK])
    n4 = each(lambda q: a2[q][CHUNK:])
    a4 = each(lambda q: mm(rows2(s2[q], n4[q]), n4[q]))
    s4 = each(lambda q: s2[q] + a4[q][:CHUNK])
    n8 = each(lambda q: a4[q][CHUNK:])
    t16 = each(lambda q: s4[q] + mm(s4[q], n8[q]))
    lo1 = each(lambda q: jnp.where(strict & same32 & jnp.logical_not(same16), l_raw[q], 0.0))
    x1 = each(lambda q: mm(t16[q], lo1[q]))
    t32 = each(lambda q: t16[q] - mm(x1[q], t16[q]))
    lo2 = each(lambda q: jnp.where(strict & jnp.logical_not(same32), l_raw[q], 0.0))
    x2 = each(lambda q: mm(t32[q], lo2[q]))
    t = each(lambda q: t32[q] - mm(x2[q], t32[q]))
    mv = each(lambda q: mm(m[q], sl("v", q)))
    z = each(lambda q: mm(t[q], jnp.concatenate([sl("p", q), mv[q]], axis=1)))
    az = each(lambda q: mm(arb[q], z[q]))
    bz = each(lambda q: mm_tn(sl("bg", q), z[q]))
    rq = each(lambda q: sl("rt", q) - az[q][:, :RW_N])
    yv = each(lambda q: mm(ark[q], sl("v", q)) - az[q][:, RW_N:])
    g = each(lambda q: jnp.where(eye > 0.0, jnp.broadcast_to(sl("gd", q), (RW_N, RW_N)), 0.0) - bz[q][:, :RW_N])
    hv = each(lambda q: mm_tn(sl("kg", q), sl("v", q)) - bz[q][:, RW_N:])
    hs = list(h0s)
    ys = []
    for ci in range(len(chunks)):
        yh = [mm(rows2(rq[(ci, h)], g[(ci, h)]), hs[h]) for h in range(RW_HEADS)]
        ys.append(jnp.concatenate([yh[h][:CHUNK] + yv[(ci, h)] for h in range(RW_HEADS)], axis=1))
        hs = [yh[h][CHUNK:] + hv[(ci, h)] for h in range(RW_HEADS)]
    return ys, hs


def _head_sums(x, ones_half):
    hw = ones_half.shape[0]
    return jnp.concatenate([_dot(_bf(x[:, j * hw:(j + 1) * hw]), ones_half) for j in range(x.shape[1] // hw)], axis=1)


def _cd_kernel(*refs, tm, layer, has_vres):
    (x_ref, g_ref, wc_ref, lb_ref, ng_ref, w_ref, mur_ref, mul_ref, w0_ref, w1_ref, w2_ref, a0_ref, a1_ref, a2_ref,
     g1_ref, g2_ref, kk_ref, ka_ref, rk_ref, lng_ref, lnb_ref, ones_ref) = refs[:22]
    if has_vres:
        vf_ref, muv_ref, v0_ref, v1_ref, v2_ref = refs[22:27]
        oc_ref, o_ref = refs[27:29]
        scratch = refs[29:]
    else:
        oc_ref, o_ref, vf_out_ref = refs[22:25]
        scratch = refs[25:]
    (hq_s, hlf_s, hk_s, hv_s, hog_s, hst_ref, r_s, lw_s, k_s, v_s, kk_s, bv_s, ch_s, cp_s, st_ref) = scratch
    W = RW_HEADS * RW_N

    @pl.when(pl.program_id(1) == 0)
    def _():
        hst_ref[...] = jnp.zeros_like(hst_ref)
        st_ref[...] = jnp.zeros_like(st_ref)
        ch_s[...] = jnp.zeros_like(ch_s)
        cp_s[...] = jnp.zeros_like(cp_s)

    h = _rms(x_ref[0], g_ref[...])
    hb = _bf(h)
    p = _dot(hb, w_ref[...])
    row_d = _iota2(h.shape, 0)
    row_p = _iota2(p.shape, 0)
    hs = jnp.where(row_d == 0, ch_s[...], pltpu.roll(h, 1, 0))
    ps = jnp.where(row_p == 0, cp_s[...], pltpu.roll(p, 1, 0))
    ch_s[...] = h[tm - 1:tm, :]
    cp_s[...] = p[tm - 1:tm, :]
    dh = hs - h
    pm = p + (ps - p) * mur_ref[...]
    r = pm[:, 0:W]
    k = pm[:, W:2 * W]
    v = pm[:, 2 * W:3 * W]
    xw = _bf(h + dh * mul_ref[0:1, :])
    xa = _bf(h + dh * mul_ref[1:2, :])
    xg = _bf(h + dh * mul_ref[2:3, :])
    wl = w0_ref[...] + _dot(_bf(jnp.tanh(_dot(xw, w1_ref[...]))), w2_ref[...])
    w_log = -_softplus(-wl) - 0.5
    lw = -jnp.exp(w_log)
    if has_vres:
        xv = _bf(h + dh * muv_ref[...])
        mixv = _sigmoid(v0_ref[...] + _dot(_bf(_dot(xv, v1_ref[...])), v2_ref[...]))
        v = v + (vf_ref[0] - v) * mixv
    else:
        vf_out_ref[0] = v
    a = _sigmoid(a0_ref[...] + _dot(_bf(_dot(xa, a1_ref[...])), a2_ref[...]))
    gate = _dot(_bf(_sigmoid(_dot(xg, g1_ref[...]))), g2_ref[...])
    ones = ones_ref[...]
    kk = k * kk_ref[...]
    ssq = _head_sums(kk * kk, ones)
    kk = kk * lax.rsqrt(jnp.maximum(ssq, 1e-24))
    k = k * (1.0 + (a - 1.0) * ka_ref[...])
    r_s[...] = r
    lw_s[...] = lw
    k_s[...] = k
    v_s[...] = v
    kk_s[...] = kk
    bv_s[...] = kk * a

    hg = _hg_pieces(hb, wc_ref, lb_ref, ng_ref, oc_ref, hq_s, hlf_s, hk_s, hv_s, hog_s, hst_ref, tm=tm, layer=layer)
    next(hg)

    def tick():
        for _ in range(HG_STEPS_PER_TICK):
            next(hg, None)

    ri = _iota2((CHUNK, CHUNK), 0)
    ci = _iota2((CHUNK, CHUNK), 1)
    strict = ci < ri
    incl = ci <= ri
    same16 = (ri // 16) == (ci // 16)
    same32 = (ri // 32) == (ci // 32)
    eye = (ri == ci).astype(F32)
    masks = (strict, incl, same16, same32, eye)
    tri = _bf(incl.astype(F32))
    rows = [pl.ds(u * CHUNK, CHUNK) for u in range(tm // CHUNK)]
    ys, h1 = _rw_chunks([(r_s[rw, :], lw_s[rw, :], k_s[rw, :], v_s[rw, :], kk_s[rw, :], bv_s[rw, :]) for rw in rows],
                        [st_ref[hd] for hd in range(RW_HEADS)], tri, masks, tick)
    for hd in range(RW_HEADS):
        st_ref[hd] = h1[hd]
    for _ in hg:
        pass

    y = jnp.concatenate(ys, axis=0)
    inv_n = 1.0 / RW_N
    y_hi = _bf(y).astype(F32)
    mean = (_head_sums(y_hi, ones) + _head_sums(y - y_hi, ones)) * inv_n
    yc = y - mean
    var = _head_sums(yc * yc, ones) * inv_n
    yn = yc * lax.rsqrt(var + RW_GN_EPS) * lng_ref[...] + lnb_ref[...]
    bonus = _head_sums(r * k * rk_ref[...], ones)
    yn = yn + bonus * v
    o_ref[0] = (yn * gate).astype(o_ref.dtype)


def _mixers_cd(x, g, w_in, hg_lb, hg_norm_g, layer, w_d, mu_rkv, mu_lora, w0, w1, w2, a0, a1, a2, g1, g2,
               k_k, k_a, r_k, ln_g, ln_b, v_first, vres, tm):
    B, S, D = x.shape
    W = RW_HEADS * RW_N
    WC = HG_HEADS * HG_DK
    has_vres = vres is not None
    hid = jnp.arange(W // 2) // RW_N
    ones = (hid[:, None] == hid[None, :]).astype(BF16)
    row = lambda t: t.reshape(1, -1)
    args = [x, row(g), w_in, hg_lb, row(hg_norm_g),
            w_d, row(mu_rkv), mu_lora, row(w0), _bf(w1), _bf(w2), row(a0), _bf(a1), _bf(a2),
            _bf(g1), _bf(g2), row(k_k), row(k_a), row(r_k), row(ln_g), row(ln_b), ones]
    in_specs = [pl.BlockSpec((1, tm, D), lambda b, i: (b, i, 0))] + [_resident(t.shape) for t in args[1:]]
    in_specs[2] = _resident_cols(D, 4 * WC, 1)
    tile = lambda w: pl.BlockSpec((1, tm, w), lambda b, i: (b, i, 0))
    out_specs = [tile(WC), tile(W)]
    out_shape = [jax.ShapeDtypeStruct((B, S, WC), BF16), jax.ShapeDtypeStruct((B, S, W), BF16)]
    if has_vres:
        mu_v, v0, v1, v2 = vres
        extra = [row(mu_v), row(v0), _bf(v1), _bf(v2)]
        args += [v_first] + extra
        in_specs += [tile(W)] + [_resident(t.shape) for t in extra]
    else:
        out_specs.append(tile(W))
        out_shape.append(jax.ShapeDtypeStruct((B, S, W), F32))
    kern = functools.partial(_cd_kernel, tm=tm, layer=layer, has_vres=has_vres)
    res = pl.pallas_call(
        kern,
        grid=(B, S // tm),
        in_specs=in_specs,
        out_specs=out_specs,
        out_shape=out_shape,
        scratch_shapes=[pltpu.VMEM((tm, WC), F32) for _ in range(5)]
        + [pltpu.VMEM((HG_HEADS, HG_DK, HG_DK), F32)]
        + [pltpu.VMEM((tm, W), F32) for _ in range(6)]
        + [pltpu.VMEM((1, D), F32), pltpu.VMEM((1, 3 * W), F32), pltpu.VMEM((RW_HEADS, RW_N, RW_N), F32)],
        compiler_params=_params("parallel", "arbitrary"),
        name="mixers_cd",
    )(*args)
    if has_vres:
        return res[0], res[1], v_first
    return res[0], res[1], res[2]


def _merge_kernel(x_ref, xh_ref, g_ref, w_ref, slg_ref, slb_ref, wsp_ref, bsp_ref, cw_ref, cb_ref, clg_ref, clb_ref,
                  yc_ref, yd_ref, wg_ref, bg_ref, wb_ref, wo_ref, o_ref, buf_ref, sh_ref, ya_ref, yb_ref, *, tm, seq):
    x = x_ref[...]
    h = _bf(_rms(x, g_ref[...]))
    pa = _dot(h, w_ref[:, 0:2 * BRANCH_W])
    buf_ref[CONV_HALO:, :] = pa[:, :BRANCH_W] * _sigmoid(pa[:, BRANCH_W:])
    first = (pl.program_id(0) * tm) % seq == 0
    pah = _dot(_bf(_rms(xh_ref[...], g_ref[...])), w_ref[:, 0:2 * BRANCH_W])
    buf_ref[0:CONV_HALO, :] = jnp.where(first, 0.0, pah[:, :BRANCH_W] * _sigmoid(pah[:, BRANCH_W:]))
    conv = _conv_pieces(cw_ref, cb_ref, clg_ref, clb_ref, buf_ref, sh_ref, ya_ref, tm=tm)
    per_tick = -(-(SUBLANES - 1 + tm // CONV_RB) // 7)

    def tick():
        for _ in range(per_tick):
            next(conv, None)

    gate_of = lambda j: _sigmoid(_dot(h, wg_ref[j]) + bg_ref[j])
    mixed = None

    def add_branch(j, y_ref):
        nonlocal mixed
        gate = gate_of(j)
        tick()
        t = gate * _dot(y_ref[...], wb_ref[j])
        tick()
        mixed = t if mixed is None else mixed + t

    add_branch(2, yc_ref)
    add_branch(3, yd_ref)
    pb = _dot(h, w_ref[:, 2 * BRANCH_W:4 * BRANCH_W])
    tick()
    u = _gelu(pb[:, :BRANCH_W])
    v = _bf(_layer_norm(_gelu(pb[:, BRANCH_W:]), slg_ref[...], slb_ref[...]))
    row = _iota2((SGU_WIN, SGU_WIN), 0)
    col = _iota2((SGU_WIN, SGU_WIN), 1)
    allowed = (col // CHUNK) <= (row // CHUNK)
    gw = BRANCH_W // SGU_GROUPS
    for g in range(SGU_GROUPS):
        wg = _bf(jnp.where(allowed, wsp_ref[g], 0.0))
        bg = bsp_ref[g]
        for w in range(tm // SGU_WIN):
            rs = slice(w * SGU_WIN, (w + 1) * SGU_WIN)
            cs = slice(g * gw, (g + 1) * gw)
            sg_mix = _dot(wg, v[rs, cs]) + bg
            yb_ref[rs, cs] = (u[rs, cs] * sg_mix).astype(yb_ref.dtype)
    add_branch(1, yb_ref)
    gate = gate_of(0)
    for _ in conv:
        pass
    mixed = mixed + gate * _dot(ya_ref[...], wb_ref[0])
    o_ref[...] = x + _dot(_bf(mixed), wo_ref[...])


def _merge(x2, g, w_in, sgu_ln_g, sgu_ln_b, sgu_w, sgu_b, conv_w, conv_b, conv_ln_g, conv_ln_b, ys,
           w_gate, b_gate, w_branch, w_out, seq, tm):
    T, D = x2.shape
    C = BRANCH_W
    tile = lambda w: pl.BlockSpec((tm, w), lambda i: (i, 0))
    hb = tm // CONV_HALO
    row = lambda t: t.reshape(1, -1)
    kern = functools.partial(_merge_kernel, tm=tm, seq=seq)
    return pl.pallas_call(
        kern,
        grid=(T // tm,),
        in_specs=[tile(D), pl.BlockSpec((CONV_HALO, D), lambda i: (jnp.maximum(i * hb - 1, 0), 0)),
                  _resident((1, D)), _resident_cols(D, 4 * C, 0),
                  _resident((1, C)), _resident((1, C)), _resident(sgu_w.shape), _resident((SGU_GROUPS, SGU_WIN, 1)),
                  _resident((CONV_WIDTH, SUBLANES, C)), _resident((1, C)), _resident((1, C)), _resident((1, C))]
        + [tile(C)] * 2
        + [_resident(w_gate.shape), _resident(b_gate.shape), _resident(w_branch.shape), _resident(w_out.shape)],
        out_specs=tile(D),
        out_shape=jax.ShapeDtypeStruct((T, D), F32),
        scratch_shapes=[pltpu.VMEM((tm + CONV_HALO, C), F32), pltpu.VMEM((SUBLANES, tm + CONV_HALO, C), F32),
                        pltpu.VMEM((tm, C), BF16), pltpu.VMEM((tm, C), BF16)],
        compiler_params=_params("parallel"),
        name="merge",
    )(x2, x2, row(g), w_in, row(sgu_ln_g), row(sgu_ln_b), sgu_w, sgu_b.reshape(SGU_GROUPS, SGU_WIN, 1),
      jnp.broadcast_to(conv_w[:, None, :], (CONV_WIDTH, SUBLANES, C)), row(conv_b), row(conv_ln_g), row(conv_ln_b),
      *ys, w_gate, b_gate, w_branch, w_out)


def _ffn_kernel(x_ref, g_ref, wg_ref, wu_ref, wd_ref, fg_ref, o_ref, *, hc, final):
    x = x_ref[...]
    h = _bf(_rms(x, g_ref[...]))
    hidden = wg_ref.shape[1]
    acc = x
    for c in range(hidden // hc):
        cs = slice(c * hc, (c + 1) * hc)
        act = _silu(_dot(h, wg_ref[:, cs])) * _dot(h, wu_ref[:, cs])
        acc = acc + _dot(_bf(act), wd_ref[cs, :])
    if final:
        acc = _rms(acc, fg_ref[...])
    o_ref[...] = acc


def _ffn(x2, g, w_gate, w_up, w_down, final_g, final, tm):
    T, D = x2.shape
    hidden = w_gate.shape[1]
    kern = functools.partial(_ffn_kernel, hc=256, final=final)
    return pl.pallas_call(
        kern,
        grid=(T // tm,),
        in_specs=[pl.BlockSpec((tm, D), lambda i: (i, 0)), _resident((1, D)),
                  _resident(w_gate.shape), _resident(w_up.shape), _resident(w_down.shape), _resident((1, D))],
        out_specs=pl.BlockSpec((tm, D), lambda i: (i, 0)),
        out_shape=jax.ShapeDtypeStruct((T, D), F32),
        compiler_params=_params("parallel"),
        name="ffn",
    )(x2, g.reshape(1, D), w_gate, w_up, w_down, final_g.reshape(1, D))


def kernel(x, norm_mix_g, w_in, w_gate, b_gate, conv_w, conv_b, conv_ln_g, conv_ln_b, sgu_ln_g, sgu_ln_b, sgu_w, sgu_b, hg_lb, hg_norm_g, rw_mu_rkv, rw_mu_lora, rw_w0, rw_w1, rw_w2, rw_a0, rw_a1, rw_a2, rw_g1, rw_g2, rw_k_k, rw_k_a, rw_r_k, rw_ln_g, rw_ln_b, rw_mu_vres, rw_v0, rw_v1, rw_v2, w_branch, w_out, norm_ffn_g, w_ffn_gate, w_ffn_up, w_ffn_down, final_norm_g):
    B, S, D = x.shape
    depth = w_in.shape[0]
    T = B * S
    tm = 256
    tm_mm = 512
    n_c = 4 * BRANCH_W + 4 * HG_HEADS * HG_DK
    v_first = None
    for l in range(depth):
        w_in_l = _bf(w_in[l])
        x2 = x.reshape(T, D)
        vres = None if l == 0 else (rw_mu_vres[l - 1], rw_v0[l - 1], rw_v1[l - 1], rw_v2[l - 1])
        y_c, y_d, v_first = _mixers_cd(x, norm_mix_g[l], w_in_l, hg_lb, hg_norm_g[l], l,
                                       w_in_l[:, n_c:], rw_mu_rkv[l], rw_mu_lora[l],
                                       rw_w0[l], rw_w1[l], rw_w2[l], rw_a0[l], rw_a1[l], rw_a2[l],
                                       rw_g1[l], rw_g2[l], rw_k_k[l], rw_k_a[l], rw_r_k[l], rw_ln_g[l], rw_ln_b[l],
                                       v_first, vres, tm)
        x2 = _merge(x2, norm_mix_g[l], w_in_l, sgu_ln_g[l], sgu_ln_b[l], sgu_w[l], sgu_b[l],
                    conv_w[l], conv_b[l], conv_ln_g[l], conv_ln_b[l], (y_c.reshape(T, -1), y_d.reshape(T, -1)),
                    _bf(w_gate[l]), b_gate[l][:, None, :], _bf(w_branch[l]), _bf(w_out[l]), S, tm_mm)
        x2 = _ffn(x2, norm_ffn_g[l], _bf(w_ffn_gate[l]), _bf(w_ffn_up[l]), _bf(w_ffn_down[l]),
                  final_norm_g, l == depth - 1, tm_mm)
        x = x2.reshape(B, S, D)
    return x
```
